```python
import math
import jax, jax.numpy as jnp
from jax import lax
import numpy as np

D_MODEL = 2048
BATCH = 4
SEQ = 8192
DEPTH = 1

EPS = 1e-6
GM_HEADS = 8
GM_HEAD_DIM = 128
GM_W = GM_HEADS * GM_HEAD_DIM
GM_CHUNK = 128
ML_HEADS = 4
ML_QK_DIM = 128
ML_V_DIM = 256
ML_QK_W = ML_HEADS * ML_QK_DIM
ML_W = ML_HEADS * ML_V_DIM
ML_CHUNK = 128
ML_CONV = 4
MIX_W = GM_W + ML_W
SPLITS = [GM_W, GM_W, ML_QK_W, ML_QK_W, ML_W, ML_W, ML_HEADS, ML_HEADS]
PROJ_W = sum(SPLITS)
PEER_HEADS = 8
PEER_QDIM = 128
PEER_HALF = PEER_QDIM // 2
N_KEYS = 128
N_EXPERTS = N_KEYS * N_KEYS
PEER_TOPK = 16
PEER_TOKENS = 128

kernel_name = "hymba_gmlp_mlstm_peer_layer"


def rmsnorm(x, g):
    xf = x.astype(jnp.float32)
    y = xf * lax.rsqrt(jnp.mean(xf * xf, axis=-1, keepdims=True) + EPS)
    return (y * g.astype(jnp.float32)).astype(x.dtype)


def head_rmsnorm(y, g, n_heads):
    shp = y.shape
    yh = y.reshape(shp[:-1] + (n_heads, shp[-1] // n_heads))
    return rmsnorm(yh, g.reshape(n_heads, -1)).reshape(shp)


def causal_depthwise_conv(x, w, b):
    c = x.shape[-1]
    y = lax.conv_general_dilated(x, w[:, None, :].astype(x.dtype), window_strides=(1,),
                                 padding=[(w.shape[0] - 1, 0)],
                                 dimension_numbers=("NWC", "WIO", "NWC"),
                                 feature_group_count=c)
    return y + b.astype(x.dtype)


def mlstm_chunk_step(carry, inp):
    C, n, m = carry
    q, k, v, ig, lf = inp
    L = q.shape[2]
    causal = jnp.tril(jnp.ones((L, L), dtype=bool))
    b = jnp.cumsum(lf, axis=-1)
    dmat = b[..., :, None] - b[..., None, :] + ig[..., None, :]
    dmat = jnp.where(causal, dmat, -jnp.inf)
    inter = b + m[..., None]
    m_t = jnp.maximum(inter, jnp.max(dmat, axis=-1))
    w_intra = jnp.exp(dmat - m_t[..., None])
    a_inter = jnp.exp(inter - m_t)
    s = jnp.einsum("bhtd,bhsd->bhts", q, k) * w_intra
    num = jnp.einsum("bhts,bhsv->bhtv", s, v) + a_inter[..., None] * jnp.einsum("bhtd,bhdv->bhtv", q, C)
    den = jnp.sum(s, axis=-1) + a_inter * jnp.einsum("bhtd,bhd->bht", q, n)
    h = num / jnp.maximum(jnp.abs(den), jnp.exp(-m_t))[..., None]
    b_end = b[..., -1]
    g = b_end[..., None] - b + ig
    m_new = jnp.maximum(b_end + m, jnp.max(g, axis=-1))
    ws = jnp.exp(g - m_new[..., None])
    ac = jnp.exp(b_end + m - m_new)
    C_new = ac[..., None, None] * C + jnp.einsum("bhs,bhsd,bhsv->bhdv", ws, k, v)
    n_new = ac[..., None] * n + jnp.einsum("bhs,bhsd->bhd", ws, k)
    return (C_new, n_new, m_new), h


def setup_inputs(seed: int = 0) -> dict:
    key = jax.random.key(seed)
    ks = jax.random.split(key, 24)
    f32 = jnp.float32
    nrm = lambda k, shp, s: jax.random.normal(k, shp, f32) * s
    gain = lambda k, n: 1.0 + 0.02 * jax.random.normal(k, (n,), f32)
    tri = jnp.tril(jnp.ones((GM_CHUNK, GM_CHUNK), f32))
    return {
        "x": jax.random.normal(ks[0], (BATCH, SEQ, D_MODEL), f32),
        "norm1_g": gain(ks[1], D_MODEL),
        "w_in": nrm(ks[2], (D_MODEL, PROJ_W), D_MODEL ** -0.5),
        "gm_vnorm_g": gain(ks[3], GM_W),
        "w_spatial": nrm(ks[4], (GM_HEADS, GM_CHUNK, GM_CHUNK), 0.5 * GM_CHUNK ** -0.5) * tri,
        "b_spatial": 1.0 + 0.01 * jax.random.normal(ks[5], (GM_HEADS, GM_CHUNK), f32),
        "ml_conv_w": nrm(ks[6], (ML_CONV, 2 * ML_QK_W), ML_CONV ** -0.5),
        "ml_conv_b": nrm(ks[7], (2 * ML_QK_W,), 0.01),
        "ml_b_i": nrm(ks[8], (ML_HEADS,), 0.1),
        "ml_b_f": jnp.linspace(3.0, 6.0, ML_HEADS, dtype=f32) + nrm(ks[9], (ML_HEADS,), 0.1),
        "gm_out_g": gain(ks[10], GM_W),
        "ml_out_g": gain(ks[11], ML_W),
        "w_out": nrm(ks[12], (MIX_W, D_MODEL), MIX_W ** -0.5),
        "norm2_g": gain(ks[13], D_MODEL),
        "peer_wq": nrm(ks[14], (D_MODEL, PEER_HEADS * PEER_QDIM), D_MODEL ** -0.5),
        "peer_k1": nrm(ks[15], (PEER_HEADS, N_KEYS, PEER_HALF), PEER_HALF ** -0.5),
        "peer_k2": nrm(ks[16], (PEER_HEADS, N_KEYS, PEER_HALF), PEER_HALF ** -0.5),
        "peer_u": nrm(ks[17], (N_EXPERTS, D_MODEL), D_MODEL ** -0.5),
        "peer_v": nrm(ks[18], (N_EXPERTS, D_MODEL), (PEER_HEADS * PEER_TOPK) ** -0.5),
        "final_g": gain(ks[19], D_MODEL),
    }


def reference(x, norm1_g, w_in, gm_vnorm_g, w_spatial, b_spatial, ml_conv_w, ml_conv_b,
              ml_b_i, ml_b_f, gm_out_g, ml_out_g, w_out, norm2_g, peer_wq, peer_k1,
              peer_k2, peer_u, peer_v, final_g):
    B, S, D = x.shape
    f32 = jnp.float32
    for _layer in range(DEPTH):
        h = rmsnorm(x, norm1_g)
        proj = h @ w_in.astype(h.dtype)
        idx = np.cumsum(SPLITS)[:-1].tolist()
        p_u, p_v, p_q, p_k, p_mv, p_o, p_i, p_f = jnp.split(proj, idx, axis=-1)

        nc_g = S // GM_CHUNK
        u = jax.nn.gelu(p_u)
        vg = rmsnorm(jax.nn.gelu(p_v), gm_vnorm_g)
        vc = vg.reshape(B, nc_g, GM_CHUNK, GM_HEADS, GM_HEAD_DIM)
        causal = jnp.tril(jnp.ones((GM_CHUNK, GM_CHUNK), dtype=bool))
        ws_m = jnp.where(causal, w_spatial, 0.0).astype(vc.dtype)
        mixed = jnp.einsum("hts,bcshd->bcthd", ws_m, vc) + b_spatial.T.astype(vc.dtype)[None, None, :, :, None]
        y_gm = u * mixed.reshape(B, S, GM_W)

        qk = jax.nn.silu(causal_depthwise_conv(jnp.concatenate([p_q, p_k], axis=-1), ml_conv_w, ml_conv_b))
        q, k = jnp.split(qk.astype(f32), 2, axis=-1)
        q = q.reshape(B, S, ML_HEADS, ML_QK_DIM)
        k = k.reshape(B, S, ML_HEADS, ML_QK_DIM) * (ML_QK_DIM ** -0.5)
        v = p_mv.astype(f32).reshape(B, S, ML_HEADS, ML_V_DIM)
        ig = p_i.astype(f32) + ml_b_i.astype(f32)
        lf = jax.nn.log_sigmoid(p_f.astype(f32) + ml_b_f.astype(f32))
        nc_m = S // ML_CHUNK
        to_c = lambda a: jnp.transpose(a.reshape(B, nc_m, ML_CHUNK, ML_HEADS, a.shape[-1]), (1, 0, 3, 2, 4))
        to_cg = lambda a: jnp.transpose(a.reshape(B, nc_m, ML_CHUNK, ML_HEADS), (1, 0, 3, 2))
        init = (jnp.zeros((B, ML_HEADS, ML_QK_DIM, ML_V_DIM), f32),
                jnp.zeros((B, ML_HEADS, ML_QK_DIM), f32),
                jnp.zeros((B, ML_HEADS), f32))
        _, hs = lax.scan(mlstm_chunk_step, init, (to_c(q), to_c(k), to_c(v), to_cg(ig), to_cg(lf)))
        h_ml = jnp.transpose(hs, (1, 0, 3, 2, 4)).reshape(B, S, ML_W).astype(x.dtype)
        y_ml = jax.nn.sigmoid(p_o) * h_ml

        y_mix = jnp.concatenate([head_rmsnorm(y_gm, gm_out_g, GM_HEADS),
                                 head_rmsnorm(y_ml, ml_out_g, ML_HEADS)], axis=-1)
        x = x + y_mix @ w_out.astype(y_mix.dtype)

        h2 = rmsnorm(x, norm2_g)
        T = B * S
        hb = h2.reshape(T // PEER_TOKENS, PEER_TOKENS, D)

        def peer_block(xc):
            qp = (xc @ peer_wq.astype(xc.dtype)).reshape(PEER_TOKENS, PEER_HEADS, PEER_QDIM)
            q1, q2 = qp[..., :PEER_HALF], qp[..., PEER_HALF:]
            s1 = jnp.einsum("thd,hnd->thn", q1, peer_k1.astype(xc.dtype)).astype(f32)
            s2 = jnp.einsum("thd,hnd->thn", q2, peer_k2.astype(xc.dtype)).astype(f32)
            v1, i1 = lax.top_k(s1, PEER_TOPK)
            v2, i2 = lax.top_k(s2, PEER_TOPK)
            cand_s = (v1[..., :, None] + v2[..., None, :]).reshape(PEER_TOKENS, PEER_HEADS, PEER_TOPK * PEER_TOPK)
            cand_i = (i1[..., :, None] * N_KEYS + i2[..., None, :]).reshape(PEER_TOKENS, PEER_HEADS, PEER_TOPK * PEER_TOPK)
            sv, pos = lax.top_k(cand_s, PEER_TOPK)
            eidx = jnp.take_along_axis(cand_i, pos, axis=-1)
            gate = jax.nn.softmax(sv, axis=-1).astype(xc.dtype)
            u_sel = peer_u.astype(xc.dtype)[eidx]
            v_sel = peer_v.astype(xc.dtype)[eidx]
            act = jax.nn.gelu(jnp.einsum("td,thkd->thk", xc, u_sel))
            return jnp.einsum("thk,thkd->td", gate * act, v_sel)

        y_peer = lax.map(peer_block, hb).reshape(B, S, D)
        x = x + y_peer
    return rmsnorm(x, final_g)
```

```python
import functools

import jax
import jax.numpy as jnp
from jax import lax
from jax.experimental import pallas as pl
from jax.experimental.pallas import tpu as pltpu

EPS = 1e-6
GM_HEADS = 8
GM_HEAD_DIM = 128
GM_W = GM_HEADS * GM_HEAD_DIM
CHUNK = 128
ML_HEADS = 4
ML_QK_DIM = 128
ML_V_DIM = 256
ML_QK_W = ML_HEADS * ML_QK_DIM
ML_W = ML_HEADS * ML_V_DIM
ML_CONV = 4
PEER_HEADS = 8
PEER_QDIM = 128
PEER_HALF = PEER_QDIM // 2
N_KEYS = 128
PEER_TOPK = 16
PEER_SLOTS = PEER_HEADS * PEER_TOPK

LANES = 128
VMEM_LIMIT = 56 * 1024 * 1024

PROJ_TILE = 256
ROUTE_TILE = 256
GATHER_TILE = 8
FINAL_TILE = 512

F32 = jnp.float32
BF16 = jnp.bfloat16
NEG_INF = float("-inf")


def _rms(x, g):
    return x * lax.rsqrt(jnp.mean(x * x, axis=-1, keepdims=True) + EPS) * g


def _dot(a, b):
    return jnp.dot(a, b, preferred_element_type=F32)


def _dot_nt(a, b):
    return lax.dot_general(a, b, (((1,), (1,)), ((), ())), preferred_element_type=F32)


def _dot_tn(a, b):
    return lax.dot_general(a, b, (((0,), (0,)), ((), ())), preferred_element_type=F32)


def _resident(shape):
    nd = len(shape)
    return pl.BlockSpec(shape, lambda *_: (0,) * nd, pipeline_mode=pl.Buffered(1))


def _in_proj_kernel(x_ref, g_ref, wm_ref, wg_ref, pu_ref, pv_ref, pqk_ref, pmv_ref, po_ref, pg_ref):
    h = _rms(x_ref[...], g_ref[...]).astype(BF16)
    for j, o_ref in enumerate((pu_ref, pv_ref, pqk_ref, pmv_ref, po_ref)):
        o_ref[...] = _dot(h, wm_ref[:, j * 1024:(j + 1) * 1024])
    pg_ref[...] = _dot(h, wg_ref[...])


def _in_proj(x2d, norm1_g, w_main, w_gate):
    T, D = x2d.shape
    tm = PROJ_TILE
    tok = lambda w: pl.BlockSpec((tm, w), lambda i: (i, 0))
    outs = [jax.ShapeDtypeStruct((T, 1024), F32)] * 5 + [jax.ShapeDtypeStruct((T, LANES), F32)]
    return pl.pallas_call(
        _in_proj_kernel,
        grid=(T // tm,),
        in_specs=[tok(D), _resident((1, D)), _resident(w_main.shape), _resident(w_gate.shape)],
        out_specs=[tok(1024)] * 5 + [tok(LANES)],
        out_shape=outs,
        compiler_params=pltpu.CompilerParams(
            dimension_semantics=("parallel",), vmem_limit_bytes=VMEM_LIMIT),
        name="in_proj",
    )(x2d, norm1_g.reshape(1, D), w_main, w_gate)


def _mixer_kernel(x_ref, pu_ref, pv_ref, pqk_ref, pmv_ref, po_ref, pg_ref,
                  gvn_ref, ws_ref, bsp_ref, cw_ref, cb_ref, bg_ref, gmo_ref, mlo_ref, wo_ref,
                  o_ref, prev_ref, c_ref, m_ref, ymix_ref):
    L = CHUNK

    @pl.when(pl.program_id(1) == 0)
    def _():
        prev_ref[...] = jnp.zeros_like(prev_ref)
        c_ref[...] = jnp.zeros_like(c_ref)
        m_ref[...] = jnp.zeros_like(m_ref)

    row = lax.broadcasted_iota(jnp.int32, (L, L), 0)
    col = lax.broadcasted_iota(jnp.int32, (L, L), 1)
    causal = col <= row

    u = jax.nn.gelu(pu_ref[...])
    vg = _rms(jax.nn.gelu(pv_ref[...]), gvn_ref[...]).astype(BF16)
    for h in range(GM_HEADS):
        sl = slice(h * GM_HEAD_DIM, (h + 1) * GM_HEAD_DIM)
        w = jnp.where(causal, ws_ref[h], 0.0).astype(BF16)
        mixed = _dot(w, vg[:, sl]) + bsp_ref[h]
        y = u[:, sl] * mixed
        ymix_ref[:, sl] = _rms(y, gmo_ref[:, sl]).astype(BF16)

    xqk = pqk_ref[...]
    prev = prev_ref[...]
    rows_w = lax.broadcasted_iota(jnp.int32, xqk.shape, 0)
    acc = cb_ref[...] + cw_ref[ML_CONV - 1:ML_CONV, :] * xqk
    for j in range(1, ML_CONV):
        shifted = jnp.where(rows_w >= j, pltpu.roll(xqk, j, 0), pltpu.roll(prev, j, 0))
        acc = acc + cw_ref[ML_CONV - 1 - j:ML_CONV - j, :] * shifted
    prev_ref[...] = xqk
    qk = acc * jax.nn.sigmoid(acc)
    q_all = qk[:, :ML_QK_W].astype(BF16)
    k_all = qk[:, ML_QK_W:] * (ML_QK_DIM ** -0.5)

    gt = pg_ref[...] + bg_ref[...]
    lf = jnp.minimum(gt, 0.0) - jnp.log1p(jnp.exp(-jnp.abs(gt)))
    bcum = lf
    sh = 1
    while sh < L:
        bcum = bcum + jnp.where(row >= sh, pltpu.roll(bcum, sh, 0), 0.0)
        sh *= 2
    gt_t = gt.T
    bcum_t = bcum.T

    ones_col = jnp.where(lax.broadcasted_iota(jnp.int32, (L, LANES), 1) == 0, 1.0, 0.0).astype(BF16)
    for h in range(ML_HEADS):
        ig_col = gt[:, h:h + 1]
        b_col = bcum[:, ML_HEADS + h:ML_HEADS + h + 1]
        ig_row = gt_t[h:h + 1, :]
        b_row = bcum_t[ML_HEADS + h:ML_HEADS + h + 1, :]
        m_prev = m_ref[h][:, 0:1]
        dmat = jnp.where(causal, b_col - b_row + ig_row, NEG_INF)
        inter = b_col + m_prev
        m_t = jnp.maximum(inter, jnp.max(dmat, axis=1, keepdims=True))
        w_intra = jnp.exp(dmat - m_t)
        a_inter = jnp.exp(inter - m_t)
        qh = q_all[:, h * ML_QK_DIM:(h + 1) * ML_QK_DIM]
        kh = k_all[:, h * ML_QK_DIM:(h + 1) * ML_QK_DIM]
        vh = pmv_ref[:, h * ML_V_DIM:(h + 1) * ML_V_DIM].astype(BF16)
        v_aug = jnp.concatenate([vh, ones_col], axis=1)
        c_aug = c_ref[h]
        s = _dot_nt(qh, kh.astype(BF16)) * w_intra
        num_aug = _dot(s.astype(BF16), v_aug) + a_inter * _dot(qh, c_aug.astype(BF16))
        num = num_aug[:, :ML_V_DIM]
        den = num_aug[:, ML_V_DIM:ML_V_DIM + 1]
        hh = num / jnp.maximum(jnp.abs(den), jnp.exp(-m_t))
        b_end = b_col[L - 1:L, :]
        g_col = b_end - b_col + ig_col
        m_new = jnp.maximum(b_end + m_prev, jnp.max(g_col, axis=0, keepdims=True))
        ws_col = jnp.exp(g_col - m_new)
        ac = jnp.exp(b_end + m_prev - m_new)
        c_ref[h] = ac * c_aug + _dot_tn((kh * ws_col).astype(BF16), v_aug)
        m_ref[h] = jnp.broadcast_to(m_new, (1, LANES))
        sl = slice(h * ML_V_DIM, (h + 1) * ML_V_DIM)
        y = jax.nn.sigmoid(po_ref[:, sl]) * hh
        ymix_ref[:, GM_W + h * ML_V_DIM:GM_W + (h + 1) * ML_V_DIM] = _rms(y, mlo_ref[:, sl]).astype(BF16)

    o_ref[...] = x_ref[...] + _dot(ymix_ref[...], wo_ref[...])


def _mixer(x2d, proj, batch, gm_vnorm_g, w_spatial, bias_sp, conv_w, conv_b, gate_b, gm_out_g, ml_out_g, w_out):
    T, D = x2d.shape
    nc = T // batch // CHUNK
    tok = lambda w: pl.BlockSpec((CHUNK, w), lambda b, c: (b * nc + c, 0))
    pu, pv, pqk, pmv, po, pg = proj
    return pl.pallas_call(
        _mixer_kernel,
        grid=(batch, nc),
        in_specs=[tok(D)] + [tok(1024)] * 5 + [tok(LANES)] + [
            _resident((1, GM_W)), _resident(w_spatial.shape), _resident(bias_sp.shape),
            _resident(conv_w.shape), _resident((1, 2 * ML_QK_W)), _resident((1, LANES)),
            _resident((1, GM_W)), _resident((1, ML_W)), _resident(w_out.shape)],
        out_specs=tok(D),
        out_shape=jax.ShapeDtypeStruct((T, D), F32),
        scratch_shapes=[
            pltpu.VMEM((CHUNK, 2 * ML_QK_W), F32),
            pltpu.VMEM((ML_HEADS, ML_QK_DIM, ML_V_DIM + LANES), F32),
            pltpu.VMEM((ML_HEADS, 1, LANES), F32),
            pltpu.VMEM((CHUNK, D), BF16),
        ],
        compiler_params=pltpu.CompilerParams(
            dimension_semantics=("arbitrary", "arbitrary"), vmem_limit_bytes=VMEM_LIMIT),
        name="mixer",
    )(x2d, pu, pv, pqk, pmv, po, pg, gm_vnorm_g.reshape(1, GM_W), w_spatial, bias_sp, conv_w,
      conv_b.reshape(1, -1), gate_b, gm_out_g.reshape(1, GM_W), ml_out_g.reshape(1, ML_W), w_out)


def _top16(s, ids=None):
    n, tm = s.shape
    rid = lax.broadcasted_iota(jnp.int32, (n, tm), 0)
    slot = lax.broadcasted_iota(jnp.int32, (PEER_TOPK, tm), 0)
    vals = jnp.zeros((PEER_TOPK, tm), F32)
    picks = jnp.zeros((PEER_TOPK, tm), jnp.int32)
    for r in range(PEER_TOPK):
        m = jnp.max(s, axis=0, keepdims=True)
        pos = jnp.min(jnp.where(s == m, rid, n), axis=0, keepdims=True)
        hit = rid == pos
        pick = pos if ids is None else jnp.sum(jnp.where(hit, ids, 0), axis=0, keepdims=True)
        s = jnp.where(hit, NEG_INF, s)
        vals = jnp.where(slot == r, m, vals)
        picks = jnp.where(slot == r, pick, picks)
    return vals, picks


def _route_kernel(x_ref, g_ref, wq_ref, k1_ref, k2_ref, h2_ref, eidx_ref, gate_ref, q_ref):
    @pl.when(pl.program_id(1) == 0)
    def _():
        h2 = _rms(x_ref[...], g_ref[...]).astype(BF16)
        h2_ref[...] = h2
        q = _dot(h2, wq_ref[...])
        for hh in range(PEER_HEADS):
            q_ref[hh] = q[:, hh * PEER_QDIM:(hh + 1) * PEER_QDIM].astype(BF16)

    qh = q_ref[pl.program_id(1)]
    s1 = _dot_nt(k1_ref[0], qh)
    s2 = _dot_nt(k2_ref[0], qh)
    v1, i1 = _top16(s1)
    v2, i2 = _top16(s2)
    cand = jnp.concatenate([v1[i:i + 1, :] + v2 for i in range(PEER_TOPK)], axis=0)
    cand_id = jnp.concatenate([i1[i:i + 1, :] * N_KEYS + i2 for i in range(PEER_TOPK)], axis=0)
    sv, eid = _top16(cand, cand_id)
    e = jnp.exp(sv - jnp.max(sv, axis=0, keepdims=True))
    gate_ref[...] = e / jnp.sum(e, axis=0, keepdims=True)
    eidx_ref[...] = eid


def _peer_route(x1, norm2_g, wq, k1p, k2p):
    T, D = x1.shape
    tm = ROUTE_TILE
    return pl.pallas_call(
        _route_kernel,
        grid=(T // tm, PEER_HEADS),
        in_specs=[pl.BlockSpec((tm, D), lambda i, h: (i, 0)), _resident((1, D)), _resident(wq.shape),
                  pl.BlockSpec((1, N_KEYS, PEER_QDIM), lambda i, h: (h, 0, 0)),
                  pl.BlockSpec((1, N_KEYS, PEER_QDIM), lambda i, h: (h, 0, 0))],
        out_specs=[pl.BlockSpec((tm, D), lambda i, h: (i, 0)),
                   pl.BlockSpec((PEER_TOPK, tm), lambda i, h: (h, i)),
                   pl.BlockSpec((PEER_TOPK, tm), lambda i, h: (h, i))],
        out_shape=[jax.ShapeDtypeStruct((T, D), BF16),
                   jax.ShapeDtypeStruct((PEER_SLOTS, T), jnp.int32),
                   jax.ShapeDtypeStruct((PEER_SLOTS, T), F32)],
        scratch_shapes=[pltpu.VMEM((PEER_HEADS, tm, PEER_QDIM), BF16)],
        compiler_params=pltpu.CompilerParams(
            dimension_semantics=("parallel", "arbitrary"), vmem_limit_bytes=VMEM_LIMIT),
        name="peer_route",
    )(x1, norm2_g.reshape(1, D), wq, k1p, k2p)


def _experts_kernel(idx_ref, idx_next_ref, gate_ref, h2_ref, tab_ref, y_ref, buf_ref, sem_ref):
    i = pl.program_id(0)
    n = pl.num_programs(0)
    slot = lax.rem(i, 2)
    rows = GATHER_TILE * PEER_SLOTS
    half = tab_ref.shape[1] // 2

    def row_copy(ids_ref, r, s):
        return pltpu.make_async_copy(tab_ref.at[pl.ds(ids_ref[0, 0, r], 1), :],
                                     buf_ref.at[s, pl.ds(r, 1), :], sem_ref.at[s])

    def issue(ids_ref, s):
        def body(r, carry):
            row_copy(ids_ref, r, s).start()
            return carry
        lax.fori_loop(0, rows, body, 0, unroll=8)

    @pl.when(i == 0)
    def _():
        issue(idx_ref, 0)

    @pl.when(i + 1 < n)
    def _():
        issue(idx_next_ref, 1 - slot)

    pltpu.make_async_copy(tab_ref.at[pl.ds(0, rows), :], buf_ref.at[slot], sem_ref.at[slot]).wait()

    lane = lax.broadcasted_iota(jnp.int32, (1, 2 * PEER_SLOTS), 1)
    even = (lane & 1) == 0
    for t in range(GATHER_TILE):
        r0 = t * PEER_SLOTS
        ub = pltpu.bitcast(buf_ref[slot, r0:r0 + PEER_SLOTS, 0:half], BF16)
        vb = pltpu.bitcast(buf_ref[slot, r0:r0 + PEER_SLOTS, half:2 * half], BF16)
        x2 = h2_ref[t]
        r = _dot_nt(x2, ub)
        a = jnp.where(even, r[0:1, :], r[1:2, :])
        act = a + jnp.where(even, pltpu.roll(a, 2 * PEER_SLOTS - 1, 1), pltpu.roll(a, 1, 1))
        w = gate_ref[t:t + 1, :] * jax.nn.gelu(act)
        lhs = jnp.concatenate([jnp.where(even, w, 0.0), jnp.where(even, 0.0, w)], axis=0).astype(BF16)
        y_ref[t] = _dot(lhs, vb)


def _peer_experts(idx3, gate2, h2p, table):
    nt = idx3.shape[0]
    T = gate2.shape[0]
    tt = GATHER_TILE
    half = table.shape[1] // 2
    rows = tt * PEER_SLOTS
    return pl.pallas_call(
        _experts_kernel,
        grid=(nt,),
        in_specs=[pl.BlockSpec((1, 1, rows), lambda i: (i, 0, 0), memory_space=pltpu.SMEM),
                  pl.BlockSpec((1, 1, rows), lambda i: (jnp.minimum(i + 1, nt - 1), 0, 0),
                               memory_space=pltpu.SMEM),
                  pl.BlockSpec((tt, 2 * PEER_SLOTS), lambda i: (i, 0)),
                  pl.BlockSpec((tt, 2, half), lambda i: (i, 0, 0)),
                  pl.BlockSpec(memory_space=pl.ANY)],
        out_specs=pl.BlockSpec((tt, 2, half), lambda i: (i, 0, 0)),
        out_shape=jax.ShapeDtypeStruct((T, 2, half), F32),
        scratch_shapes=[pltpu.VMEM((2, rows, 2 * half), jnp.uint32),
                        pltpu.SemaphoreType.DMA((2,))],
        compiler_params=pltpu.CompilerParams(
            dimension_semantics=("arbitrary",), vmem_limit_bytes=VMEM_LIMIT),
        name="peer_experts",
    )(idx3, idx3, gate2, h2p, table)


def _final_kernel(x_ref, y_ref, g_ref, o_ref):
    o_ref[...] = _rms(x_ref[...] + y_ref[...], g_ref[...])


def _final_norm(x1, y, final_g):
    T, D = x1.shape
    tm = FINAL_TILE
    tok = pl.BlockSpec((tm, D), lambda i: (i, 0))
    return pl.pallas_call(
        _final_kernel,
        grid=(T // tm,),
        in_specs=[tok, tok, _resident((1, D))],
        out_specs=tok,
        out_shape=jax.ShapeDtypeStruct((T, D), F32),
        compiler_params=pltpu.CompilerParams(
            dimension_semantics=("parallel",), vmem_limit_bytes=VMEM_LIMIT),
        name="final_norm",
    )(x1, y, final_g.reshape(1, D))


def _pack_halves(w):
    half = w.shape[1] // 2
    b = lax.bitcast_convert_type(w.astype(BF16), jnp.uint16).astype(jnp.uint32)
    return b[:, :half] | (b[:, half:] << 16)


def kernel(x, norm1_g, w_in, gm_vnorm_g, w_spatial, b_spatial, ml_conv_w, ml_conv_b, ml_b_i, ml_b_f,
           gm_out_g, ml_out_g, w_out, norm2_g, peer_wq, peer_k1, peer_k2, peer_u, peer_v, final_g):
    B, S, D = x.shape
    T = B * S
    assert S % CHUNK == 0 and T % FINAL_TILE == 0 and D % (2 * LANES) == 0
    x2d = x.reshape(T, D)

    n_main = 2 * GM_W + 2 * ML_QK_W + 2 * ML_W
    w_main = w_in[:, :n_main].astype(BF16)
    w_gate = jnp.pad(w_in[:, n_main:], ((0, 0), (0, LANES - 2 * ML_HEADS))).astype(BF16)
    gate_b = jnp.pad(jnp.concatenate([ml_b_i, ml_b_f]), (0, LANES - 2 * ML_HEADS)).reshape(1, LANES)
    bias_sp = jnp.broadcast_to(b_spatial[:, :, None], (GM_HEADS, CHUNK, GM_HEAD_DIM))
    k1p = jnp.pad(peer_k1, ((0, 0), (0, 0), (0, PEER_HALF))).astype(BF16)
    k2p = jnp.pad(peer_k2, ((0, 0), (0, 0), (PEER_HALF, 0))).astype(BF16)
    table = jnp.concatenate([_pack_halves(peer_u), _pack_halves(peer_v)], axis=1)

    proj = _in_proj(x2d, norm1_g, w_main, w_gate)
    x1 = _mixer(x2d, proj, B, gm_vnorm_g, w_spatial, bias_sp, ml_conv_w, ml_conv_b, gate_b,
                gm_out_g, ml_out_g, w_out.astype(BF16))
    h2, eidx_t, gate_t = _peer_route(x1, norm2_g, peer_wq.astype(BF16), k1p, k2p)
    idx3 = eidx_t.T.reshape(T // GATHER_TILE, 1, GATHER_TILE * PEER_SLOTS)
    gate2 = jnp.repeat(gate_t.T, 2, axis=1)
    y = _peer_experts(idx3, gate2, h2.reshape(T, 2, D // 2), table)
    out = _final_norm(x1, y.reshape(T, D), final_g)
    return out.reshape(B, S, D)
```

```python
import functools

import jax
import jax.numpy as jnp
from jax import lax
from jax.experimental import pallas as pl
from jax.experimental.pallas import tpu as pltpu

EPS = 1e-6
GM_HEADS = 8
GM_HEAD_DIM = 128
GM_W = GM_HEADS * GM_HEAD_DIM
CHUNK = 128
ML_HEADS = 4
ML_QK_DIM = 128
ML_V_DIM = 256
ML_QK_W = ML_HEADS * ML_QK_DIM
ML_W = ML_HEADS * ML_V_DIM
ML_CONV = 4
PEER_HEADS = 8
PEER_QDIM = 128
PEER_HALF = PEER_QDIM // 2
N_KEYS = 128
PEER_TOPK = 16
PEER_SLOTS = PEER_HEADS * PEER_TOPK

LANES = 128
SUBLANES = 8
VMEM_LIMIT = 56 * 1024 * 1024

PROJ_TILE = 256
ROUTE_TILE = 256
GATHER_TILE = 16
GATHER_BUFS = 4
FINAL_TILE = 512

F32 = jnp.float32
BF16 = jnp.bfloat16
NEG_INF = float("-inf")


def _rms(x, g):
    return x * lax.rsqrt(jnp.mean(x * x, axis=-1, keepdims=True) + EPS) * g


def _dot(a, b):
    return jnp.dot(a, b, preferred_element_type=F32)


def _dot_nt(a, b):
    return lax.dot_general(a, b, (((1,), (1,)), ((), ())), preferred_element_type=F32)


def _dot_tn(a, b):
    return lax.dot_general(a, b, (((0,), (0,)), ((), ())), preferred_element_type=F32)


def _resident(shape):
    nd = len(shape)
    return pl.BlockSpec(shape, lambda *_: (0,) * nd, pipeline_mode=pl.Buffered(1))


def _in_proj_kernel(x_ref, g_ref, wm_ref, wg_ref, pu_ref, pv_ref, pqk_ref, pmv_ref, po_ref, pg_ref):
    h = _rms(x_ref[...], g_ref[...]).astype(BF16)
    for j, o_ref in enumerate((pu_ref, pv_ref, pqk_ref, pmv_ref, po_ref)):
        o_ref[...] = _dot(h, wm_ref[:, j * 1024:(j + 1) * 1024])
    pg_ref[...] = _dot(h, wg_ref[...])


def _in_proj(x2d, norm1_g, w_main, w_gate):
    T, D = x2d.shape
    tm = PROJ_TILE
    tok = lambda w: pl.BlockSpec((tm, w), lambda i: (i, 0))
    outs = [jax.ShapeDtypeStruct((T, 1024), F32)] * 5 + [jax.ShapeDtypeStruct((T, LANES), F32)]
    return pl.pallas_call(
        _in_proj_kernel,
        grid=(T // tm,),
        in_specs=[tok(D), _resident((1, D)), _resident(w_main.shape), _resident(w_gate.shape)],
        out_specs=[tok(1024)] * 5 + [tok(LANES)],
        out_shape=outs,
        compiler_params=pltpu.CompilerParams(
            dimension_semantics=("parallel",), vmem_limit_bytes=VMEM_LIMIT),
        name="in_proj",
    )(x2d, norm1_g.reshape(1, D), w_main, w_gate)


def _mixer_kernel(x_ref, pu_ref, pv_ref, pqk_ref, pmv_ref, po_ref, pg_ref,
                  gvn_ref, ws_ref, bsp_ref, cw_ref, cb_ref, bg_ref, gmo_ref, mlo_ref, wo_ref,
                  o_ref, prev_ref, c_ref, m_ref, ymix_ref):
    L = CHUNK

    @pl.when(pl.program_id(1) == 0)
    def _():
        prev_ref[...] = jnp.zeros_like(prev_ref)
        c_ref[...] = jnp.zeros_like(c_ref)
        m_ref[...] = jnp.zeros_like(m_ref)

    row = lax.broadcasted_iota(jnp.int32, (L, L), 0)
    col = lax.broadcasted_iota(jnp.int32, (L, L), 1)
    causal = col <= row

    u = jax.nn.gelu(pu_ref[...])
    vg = _rms(jax.nn.gelu(pv_ref[...]), gvn_ref[...]).astype(BF16)
    for h in range(GM_HEADS):
        sl = slice(h * GM_HEAD_DIM, (h + 1) * GM_HEAD_DIM)
        w = jnp.where(causal, ws_ref[h], 0.0).astype(BF16)
        mixed = _dot(w, vg[:, sl]) + bsp_ref[h]
        y = u[:, sl] * mixed
        ymix_ref[:, sl] = _rms(y, gmo_ref[:, sl]).astype(BF16)

    xqk = pqk_ref[...]
    prev = prev_ref[...]
    rows_w = lax.broadcasted_iota(jnp.int32, xqk.shape, 0)
    acc = cb_ref[...] + cw_ref[ML_CONV - 1:ML_CONV, :] * xqk
    for j in range(1, ML_CONV):
        shifted = jnp.where(rows_w >= j, pltpu.roll(xqk, j, 0), pltpu.roll(prev, j, 0))
        acc = acc + cw_ref[ML_CONV - 1 - j:ML_CONV - j, :] * shifted
    prev_ref[...] = xqk
    qk = acc * jax.nn.sigmoid(acc)
    q_all = qk[:, :ML_QK_W].astype(BF16)
    k_all = qk[:, ML_QK_W:] * (ML_QK_DIM ** -0.5)

    gt = pg_ref[...] + bg_ref[...]
    lf = jnp.minimum(gt, 0.0) - jnp.log1p(jnp.exp(-jnp.abs(gt)))
    bcum = lf
    sh = 1
    while sh < L:
        bcum = bcum + jnp.where(row >= sh, pltpu.roll(bcum, sh, 0), 0.0)
        sh *= 2
    gt_t = gt.T
    bcum_t = bcum.T

    ones_col = jnp.where(lax.broadcasted_iota(jnp.int32, (L, LANES), 1) == 0, 1.0, 0.0).astype(BF16)
    for h in range(ML_HEADS):
        ig_col = gt[:, h:h + 1]
        b_col = bcum[:, ML_HEADS + h:ML_HEADS + h + 1]
        ig_row = gt_t[h:h + 1, :]
        b_row = bcum_t[ML_HEADS + h:ML_HEADS + h + 1, :]
        m_prev = m_ref[h][:, 0:1]
        dmat = jnp.where(causal, b_col - b_row + ig_row, NEG_INF)
        inter = b_col + m_prev
        m_t = jnp.maximum(inter, jnp.max(dmat, axis=1, keepdims=True))
        w_intra = jnp.exp(dmat - m_t)
        a_inter = jnp.exp(inter - m_t)
        qh = q_all[:, h * ML_QK_DIM:(h + 1) * ML_QK_DIM]
        kh = k_all[:, h * ML_QK_DIM:(h + 1) * ML_QK_DIM]
        vh = pmv_ref[:, h * ML_V_DIM:(h + 1) * ML_V_DIM].astype(BF16)
        v_aug = jnp.concatenate([vh, ones_col], axis=1)
        c_aug = c_ref[h]
        s = _dot_nt(qh, kh.astype(BF16)) * w_intra
        num_aug = _dot(s.astype(BF16), v_aug) + a_inter * _dot(qh, c_aug.astype(BF16))
        num = num_aug[:, :ML_V_DIM]
        den = num_aug[:, ML_V_DIM:ML_V_DIM + 1]
        hh = num / jnp.maximum(jnp.abs(den), jnp.exp(-m_t))
        b_end = b_col[L - 1:L, :]
        g_col = b_end - b_col + ig_col
        m_new = jnp.maximum(b_end + m_prev, jnp.max(g_col, axis=0, keepdims=True))
        ws_col = jnp.exp(g_col - m_new)
        ac = jnp.exp(b_end + m_prev - m_new)
        c_ref[h] = ac * c_aug + _dot_tn((kh * ws_col).astype(BF16), v_aug)
        m_ref[h] = jnp.broadcast_to(m_new, (1, LANES))
        sl = slice(h * ML_V_DIM, (h + 1) * ML_V_DIM)
        y = jax.nn.sigmoid(po_ref[:, sl]) * hh
        ymix_ref[:, GM_W + h * ML_V_DIM:GM_W + (h + 1) * ML_V_DIM] = _rms(y, mlo_ref[:, sl]).astype(BF16)

    o_ref[...] = x_ref[...] + _dot(ymix_ref[...], wo_ref[...])


def _mixer(x2d, proj, batch, gm_vnorm_g, w_spatial, bias_sp, conv_w, conv_b, gate_b, gm_out_g, ml_out_g, w_out):
    T, D = x2d.shape
    nc = T // batch // CHUNK
    tok = lambda w: pl.BlockSpec((CHUNK, w), lambda b, c: (b * nc + c, 0))
    pu, pv, pqk, pmv, po, pg = proj
    return pl.pallas_call(
        _mixer_kernel,
        grid=(batch, nc),
        in_specs=[tok(D)] + [tok(1024)] * 5 + [tok(LANES)] + [
            _resident((1, GM_W)), _resident(w_spatial.shape), _resident(bias_sp.shape),
            _resident(conv_w.shape), _resident((1, 2 * ML_QK_W)), _resident((1, LANES)),
            _resident((1, GM_W)), _resident((1, ML_W)), _resident(w_out.shape)],
        out_specs=tok(D),
        out_shape=jax.ShapeDtypeStruct((T, D), F32),
        scratch_shapes=[
            pltpu.VMEM((CHUNK, 2 * ML_QK_W), F32),
            pltpu.VMEM((ML_HEADS, ML_QK_DIM, ML_V_DIM + LANES), F32),
            pltpu.VMEM((ML_HEADS, 1, LANES), F32),
            pltpu.VMEM((CHUNK, D), BF16),
        ],
        compiler_params=pltpu.CompilerParams(
            dimension_semantics=("arbitrary", "arbitrary"), vmem_limit_bytes=VMEM_LIMIT),
        name="mixer",
    )(x2d, pu, pv, pqk, pmv, po, pg, gm_vnorm_g.reshape(1, GM_W), w_spatial, bias_sp, conv_w,
      conv_b.reshape(1, -1), gate_b, gm_out_g.reshape(1, GM_W), ml_out_g.reshape(1, ML_W), w_out)


def _top16(s, ids=None):
    n, tm = s.shape
    rid = lax.broadcasted_iota(jnp.int32, (n, tm), 0)
    slot = lax.broadcasted_iota(jnp.int32, (PEER_TOPK, tm), 0)
    vals = jnp.zeros((PEER_TOPK, tm), F32)
    picks = jnp.zeros((PEER_TOPK, tm), jnp.int32)
    for r in range(PEER_TOPK):
        m = jnp.max(s, axis=0, keepdims=True)
        pos = jnp.min(jnp.where(s == m, rid, n), axis=0, keepdims=True)
        hit = rid == pos
        pick = pos if ids is None else jnp.sum(jnp.where(hit, ids, 0), axis=0, keepdims=True)
        s = jnp.where(hit, NEG_INF, s)
        vals = jnp.where(slot == r, m, vals)
        picks = jnp.where(slot == r, pick, picks)
    return vals, picks


def _route_kernel(x_ref, g_ref, wq_ref, k1_ref, k2_ref, h2_ref, eidx_ref, gate_ref, q_ref):
    @pl.when(pl.program_id(1) == 0)
    def _():
        h2 = _rms(x_ref[...], g_ref[...]).astype(BF16)
        h2_ref[...] = h2
        q = _dot(h2, wq_ref[...])
        for hh in range(PEER_HEADS):
            q_ref[hh] = q[:, hh * PEER_QDIM:(hh + 1) * PEER_QDIM].astype(BF16)

    qh = q_ref[pl.program_id(1)]
    s1 = _dot_nt(k1_ref[0], qh)
    s2 = _dot_nt(k2_ref[0], qh)
    v1, i1 = _top16(s1)
    v2, i2 = _top16(s2)
    tm = v1.shape[1]
    sub = lax.broadcasted_iota(jnp.int32, (SUBLANES, tm), 0)
    blocks, id_blocks = [v1[0:1, :] + v2], [i1[0:1, :] * N_KEYS + i2]
    for i in range(1, SUBLANES):
        blocks.append(jnp.where(sub < PEER_TOPK // (i + 1), v1[i:i + 1, :] + v2[0:SUBLANES, :], NEG_INF))
        id_blocks.append(i1[i:i + 1, :] * N_KEYS + i2[0:SUBLANES, :])
    blocks.append(v1[SUBLANES:, :] + v2[0:1, :])
    id_blocks.append(i1[SUBLANES:, :] * N_KEYS + i2[0:1, :])
    sv, eid = _top16(jnp.concatenate(blocks, axis=0), jnp.concatenate(id_blocks, axis=0))
    e = jnp.exp(sv - jnp.max(sv, axis=0, keepdims=True))
    gate_ref[...] = e / jnp.sum(e, axis=0, keepdims=True)
    eidx_ref[...] = eid


def _peer_route(x1, norm2_g, wq, k1p, k2p):
    T, D = x1.shape
    tm = ROUTE_TILE
    return pl.pallas_call(
        _route_kernel,
        grid=(T // tm, PEER_HEADS),
        in_specs=[pl.BlockSpec((tm, D), lambda i, h: (i, 0)), _resident((1, D)), _resident(wq.shape),
                  pl.BlockSpec((1, N_KEYS, PEER_QDIM), lambda i, h: (h, 0, 0)),
                  pl.BlockSpec((1, N_KEYS, PEER_QDIM), lambda i, h: (h, 0, 0))],
        out_specs=[pl.BlockSpec((tm, D), lambda i, h: (i, 0)),
                   pl.BlockSpec((PEER_TOPK, tm), lambda i, h: (h, i)),
                   pl.BlockSpec((PEER_TOPK, tm), lambda i, h: (h, i))],
        out_shape=[jax.ShapeDtypeStruct((T, D), BF16),
                   jax.ShapeDtypeStruct((PEER_SLOTS, T), jnp.int32),
                   jax.ShapeDtypeStruct((PEER_SLOTS, T), F32)],
        scratch_shapes=[pltpu.VMEM((PEER_HEADS, tm, PEER_QDIM), BF16)],
        compiler_params=pltpu.CompilerParams(
            dimension_semantics=("parallel", "arbitrary"), vmem_limit_bytes=VMEM_LIMIT),
        name="peer_route",
    )(x1, norm2_g.reshape(1, D), wq, k1p, k2p)


def _experts_kernel(idx_ref, idx_next_ref, gate_ref, h2_ref, tab_ref, tabw_ref, y_ref, *scratch):
    bufs, sem_ref = scratch[:GATHER_BUFS], scratch[GATHER_BUFS]
    i = pl.program_id(0)
    n = pl.num_programs(0)
    gt = GATHER_TILE // GATHER_BUFS
    rows = gt * PEER_SLOTS
    nu = tab_ref.shape[1] // 2
    kt_per_tok = PEER_SLOTS // SUBLANES

    def issue(ids_ref, g):
        for r in range(rows):
            pltpu.make_async_copy(
                tab_ref.at[ids_ref[0, 0, g * rows + r]],
                bufs[g].at[r // SUBLANES, :, pl.ds(r % SUBLANES, 1), :],
                sem_ref.at[g]).start(priority=r % 2)

    def wait(g):
        pltpu.make_async_copy(tabw_ref.at[pl.ds(0, bufs[g].shape[0])], bufs[g], sem_ref.at[g]).wait()

    lane = lax.broadcasted_iota(jnp.int32, (1, 2 * PEER_SLOTS), 1)
    even = (lane & 1) == 0

    def table_half(buf, t, s0):
        words = jnp.concatenate(
            [buf[t * kt_per_tok:(t + 1) * kt_per_tok, s0 + s].reshape(PEER_SLOTS, LANES)
             for s in range(nu)], axis=1)
        return pltpu.bitcast(words, BF16)

    def compute(g):
        buf, t0 = bufs[g], g * gt
        rs = [_dot_nt(h2_ref[t0 + t], table_half(buf, t, 0)) for t in range(gt)]
        lhs = []
        for t in range(gt):
            a = jnp.where(even, rs[t][0:1, :], rs[t][1:2, :])
            act = a + jnp.where(even, pltpu.roll(a, 2 * PEER_SLOTS - 1, 1), pltpu.roll(a, 1, 1))
            w = gate_ref[t0 + t:t0 + t + 1, :] * jax.nn.gelu(act)
            lhs.append(jnp.concatenate([jnp.where(even, w, 0.0), jnp.where(even, 0.0, w)],
                                       axis=0).astype(BF16))
        for t in range(gt):
            y_ref[t0 + t] = _dot(lhs[t], table_half(buf, t, nu))

    @pl.when(i == 0)
    def _():
        for g in range(GATHER_BUFS - 1):
            issue(idx_ref, g)

    for g in range(GATHER_BUFS):
        wait(g)
        compute(g)
        ahead = g + GATHER_BUFS - 1
        if ahead < GATHER_BUFS:
            issue(idx_ref, ahead)
        else:
            issue(idx_next_ref, ahead - GATHER_BUFS)

    @pl.when(i == n - 1)
    def _():
        for g in range(GATHER_BUFS - 1):
            wait(g)


def _peer_experts(idx3, gate2, h2p, table):
    nt = idx3.shape[0] - 1
    T = gate2.shape[0]
    tt = GATHER_TILE
    half = h2p.shape[2]
    srows = table.shape[1]
    rows = tt * PEER_SLOTS
    buf = pltpu.VMEM((rows // GATHER_BUFS // SUBLANES, srows, SUBLANES, LANES), jnp.uint32)
    return pl.pallas_call(
        _experts_kernel,
        grid=(nt,),
        in_specs=[pl.BlockSpec((1, 1, rows), lambda i: (i, 0, 0), memory_space=pltpu.SMEM),
                  pl.BlockSpec((1, 1, rows), lambda i: (i + 1, 0, 0), memory_space=pltpu.SMEM),
                  pl.BlockSpec((tt, 2 * PEER_SLOTS), lambda i: (i, 0)),
                  pl.BlockSpec((tt, 2, half), lambda i: (i, 0, 0)),
                  pl.BlockSpec(memory_space=pl.ANY),
                  pl.BlockSpec(memory_space=pl.ANY)],
        out_specs=pl.BlockSpec((tt, 2, half), lambda i: (i, 0, 0)),
        out_shape=jax.ShapeDtypeStruct((T, 2, half), F32),
        scratch_shapes=[buf] * GATHER_BUFS + [pltpu.SemaphoreType.DMA((GATHER_BUFS,))],
        compiler_params=pltpu.CompilerParams(
            dimension_semantics=("arbitrary",), vmem_limit_bytes=VMEM_LIMIT),
        name="peer_experts",
    )(idx3, idx3, gate2, h2p, table.reshape(table.shape[0], srows, 1, LANES),
      table.reshape(-1, srows, SUBLANES, LANES))


def _final_kernel(x_ref, y_ref, g_ref, o_ref):
    o_ref[...] = _rms(x_ref[...] + y_ref[...], g_ref[...])


def _final_norm(x1, y, final_g):
    T, D = x1.shape
    tm = FINAL_TILE
    tok = pl.BlockSpec((tm, D), lambda i: (i, 0))
    return pl.pallas_call(
        _final_kernel,
        grid=(T // tm,),
        in_specs=[tok, tok, _resident((1, D))],
        out_specs=tok,
        out_shape=jax.ShapeDtypeStruct((T, D), F32),
        compiler_params=pltpu.CompilerParams(
            dimension_semantics=("parallel",), vmem_limit_bytes=VMEM_LIMIT),
        name="final_norm",
    )(x1, y, final_g.reshape(1, D))


def _pack_halves(w):
    half = w.shape[1] // 2
    b = lax.bitcast_convert_type(w.astype(BF16), jnp.uint16).astype(jnp.uint32)
    return b[:, :half] | (b[:, half:] << 16)


def kernel(x, norm1_g, w_in, gm_vnorm_g, w_spatial, b_spatial, ml_conv_w, ml_conv_b, ml_b_i, ml_b_f,
           gm_out_g, ml_out_g, w_out, norm2_g, peer_wq, peer_k1, peer_k2, peer_u, peer_v, final_g):
    B, S, D = x.shape
    T = B * S
    assert S % CHUNK == 0 and T % FINAL_TILE == 0 and D % (2 * LANES) == 0
    x2d = x.reshape(T, D)

    n_main = 2 * GM_W + 2 * ML_QK_W + 2 * ML_W
    w_main = w_in[:, :n_main].astype(BF16)
    w_gate = jnp.pad(w_in[:, n_main:], ((0, 0), (0, LANES - 2 * ML_HEADS))).astype(BF16)
    gate_b = jnp.pad(jnp.concatenate([ml_b_i, ml_b_f]), (0, LANES - 2 * ML_HEADS)).reshape(1, LANES)
    bias_sp = jnp.broadcast_to(b_spatial[:, :, None], (GM_HEADS, CHUNK, GM_HEAD_DIM))
    k1p = jnp.pad(peer_k1, ((0, 0), (0, 0), (0, PEER_HALF))).astype(BF16)
    k2p = jnp.pad(peer_k2, ((0, 0), (0, 0), (PEER_HALF, 0))).astype(BF16)
    table = jnp.concatenate([_pack_halves(peer_u), _pack_halves(peer_v)], axis=1)
    table = table.reshape(table.shape[0], -1, LANES)

    proj = _in_proj(x2d, norm1_g, w_main, w_gate)
    x1 = _mixer(x2d, proj, B, gm_vnorm_g, w_spatial, bias_sp, ml_conv_w, ml_conv_b, gate_b,
                gm_out_g, ml_out_g, w_out.astype(BF16))
    h2, eidx_t, gate_t = _peer_route(x1, norm2_g, peer_wq.astype(BF16), k1p, k2p)
    idx3 = eidx_t.T.reshape(T // GATHER_TILE, 1, GATHER_TILE * PEER_SLOTS)
    idx3 = jnp.pad(idx3, ((0, 1), (0, 0), (0, 0)))
    gate2 = jnp.repeat(gate_t.T, 2, axis=1)
    y = _peer_experts(idx3, gate2, h2.reshape(T, 2, D // 2), table)
    out = _final_norm(x1, y.reshape(T, D), final_g)
    return out.reshape(B, S, D)
```

```python
import functools

import jax
import jax.numpy as jnp
from jax import lax
from jax.experimental import pallas as pl
from jax.experimental.pallas import tpu as pltpu

EPS = 1e-6
GM_HEADS = 8
GM_HEAD_DIM = 128
GM_W = GM_HEADS * GM_HEAD_DIM
CHUNK = 128
ML_HEADS = 4
ML_QK_DIM = 128
ML_V_DIM = 256
ML_QK_W = ML_HEADS * ML_QK_DIM
ML_W = ML_HEADS * ML_V_DIM
ML_CONV = 4
PEER_HEADS = 8
PEER_QDIM = 128
PEER_HALF = PEER_QDIM // 2
N_KEYS = 128
PEER_TOPK = 16
PEER_SLOTS = PEER_HEADS * PEER_TOPK

LANES = 128
SUBLANES = 8
VMEM_LIMIT = 56 * 1024 * 1024

PACK_TILE = 256
PROJ_TILE = 256
ROUTE_TILE = 256
ROUTE_WIDTH = 256
GATHER_TILE = 16
GATHER_BUFS = 16
GATHER_GROUP = 2
GATHER_AHEAD = 4
assert GATHER_TILE % GATHER_BUFS == 0 and GATHER_BUFS >= (GATHER_AHEAD + 3) * GATHER_GROUP

F32 = jnp.float32
BF16 = jnp.bfloat16
NEG_INF = float("-inf")


def _rms(x, g):
    return x * lax.rsqrt(jnp.mean(x * x, axis=-1, keepdims=True) + EPS) * g


def _dot(a, b):
    return jnp.dot(a, b, preferred_element_type=F32)


def _dot_nt(a, b):
    return lax.dot_general(a, b, (((1,), (1,)), ((), ())), preferred_element_type=F32)


def _dot_tn(a, b):
    return lax.dot_general(a, b, (((0,), (0,)), ((), ())), preferred_element_type=F32)


def _resident(shape):
    nd = len(shape)
    return pl.BlockSpec(shape, lambda *_: (0,) * nd, pipeline_mode=pl.Buffered(1))


def _in_proj_kernel(x_ref, g_ref, wm_ref, wg_ref, pu_ref, pv_ref, pqk_ref, pmv_ref, po_ref, pg_ref):
    h = _rms(x_ref[...], g_ref[...]).astype(BF16)
    for j, o_ref in enumerate((pu_ref, pv_ref, pqk_ref, pmv_ref, po_ref)):
        o_ref[...] = _dot(h, wm_ref[:, j * 1024:(j + 1) * 1024])
    pg_ref[...] = _dot(h, wg_ref[...])


def _in_proj(x2d, norm1_g, w_main, w_gate):
    T, D = x2d.shape
    tm = PROJ_TILE
    tok = lambda w: pl.BlockSpec((tm, w), lambda i: (i, 0))
    outs = [jax.ShapeDtypeStruct((T, 1024), F32)] * 5 + [jax.ShapeDtypeStruct((T, LANES), F32)]
    return pl.pallas_call(
        _in_proj_kernel,
        grid=(T // tm,),
        in_specs=[tok(D), _resident((1, D)), _resident(w_main.shape), _resident(w_gate.shape)],
        out_specs=[tok(1024)] * 5 + [tok(LANES)],
        out_shape=outs,
        compiler_params=pltpu.CompilerParams(
            dimension_semantics=("parallel",), vmem_limit_bytes=VMEM_LIMIT),
        name="in_proj",
    )(x2d, norm1_g.reshape(1, D), w_main, w_gate)


def _mixer_kernel(x_ref, pu_ref, pv_ref, pqk_ref, pmv_ref, po_ref, pg_ref,
                  gvn_ref, ws_ref, bsp_ref, cw_ref, cb_ref, bg_ref, gmo_ref, mlo_ref, wo_ref,
                  o_ref, prev_ref, c_ref, m_ref, ymix_ref):
    L = CHUNK

    @pl.when(pl.program_id(1) == 0)
    def _():
        prev_ref[...] = jnp.zeros_like(prev_ref)
        c_ref[...] = jnp.zeros_like(c_ref)
        m_ref[...] = jnp.zeros_like(m_ref)

    row = lax.broadcasted_iota(jnp.int32, (L, L), 0)
    col = lax.broadcasted_iota(jnp.int32, (L, L), 1)
    causal = col <= row

    u = jax.nn.gelu(pu_ref[...])
    vg = _rms(jax.nn.gelu(pv_ref[...]), gvn_ref[...]).astype(BF16)
    for h in range(GM_HEADS):
        sl = slice(h * GM_HEAD_DIM, (h + 1) * GM_HEAD_DIM)
        w = jnp.where(causal, ws_ref[h], 0.0).astype(BF16)
        mixed = _dot(w, vg[:, sl]) + bsp_ref[h]
        y = u[:, sl] * mixed
        ymix_ref[:, sl] = _rms(y, gmo_ref[:, sl]).astype(BF16)

    xqk = pqk_ref[...]
    prev = prev_ref[...]
    rows_w = lax.broadcasted_iota(jnp.int32, xqk.shape, 0)
    acc = cb_ref[...] + cw_ref[ML_CONV - 1:ML_CONV, :] * xqk
    for j in range(1, ML_CONV):
        shifted = jnp.where(rows_w >= j, pltpu.roll(xqk, j, 0), pltpu.roll(prev, j, 0))
        acc = acc + cw_ref[ML_CONV - 1 - j:ML_CONV - j, :] * shifted
    prev_ref[...] = xqk
    qk = acc * jax.nn.sigmoid(acc)
    q_all = qk[:, :ML_QK_W].astype(BF16)
    k_all = qk[:, ML_QK_W:] * (ML_QK_DIM ** -0.5)

    gt = pg_ref[...] + bg_ref[...]
    lf = jnp.minimum(gt, 0.0) - jnp.log1p(jnp.exp(-jnp.abs(gt)))
    bcum = lf
    sh = 1
    while sh < L:
        bcum = bcum + jnp.where(row >= sh, pltpu.roll(bcum, sh, 0), 0.0)
        sh *= 2
    gt_t = gt.T
    bcum_t = bcum.T

    ones_col = jnp.where(lax.broadcasted_iota(jnp.int32, (L, LANES), 1) == 0, 1.0, 0.0).astype(BF16)
    for h in range(ML_HEADS):
        ig_col = gt[:, h:h + 1]
        b_col = bcum[:, ML_HEADS + h:ML_HEADS + h + 1]
        ig_row = gt_t[h:h + 1, :]
        b_row = bcum_t[ML_HEADS + h:ML_HEADS + h + 1, :]
        m_prev = m_ref[h][:, 0:1]
        dmat = jnp.where(causal, b_col - b_row + ig_row, NEG_INF)
        inter = b_col + m_prev
        m_t = jnp.maximum(inter, jnp.max(dmat, axis=1, keepdims=True))
        w_intra = jnp.exp(dmat - m_t)
        a_inter = jnp.exp(inter - m_t)
        qh = q_all[:, h * ML_QK_DIM:(h + 1) * ML_QK_DIM]
        kh = k_all[:, h * ML_QK_DIM:(h + 1) * ML_QK_DIM]
        vh = pmv_ref[:, h * ML_V_DIM:(h + 1) * ML_V_DIM].astype(BF16)
        v_aug = jnp.concatenate([vh, ones_col], axis=1)
        c_aug = c_ref[h]
        s = _dot_nt(qh, kh.astype(BF16)) * w_intra
        num_aug = _dot(s.astype(BF16), v_aug) + a_inter * _dot(qh, c_aug.astype(BF16))
        num = num_aug[:, :ML_V_DIM]
        den = num_aug[:, ML_V_DIM:ML_V_DIM + 1]
        hh = num / jnp.maximum(jnp.abs(den), jnp.exp(-m_t))
        b_end = b_col[L - 1:L, :]
        g_col = b_end - b_col + ig_col
        m_new = jnp.maximum(b_end + m_prev, jnp.max(g_col, axis=0, keepdims=True))
        ws_col = jnp.exp(g_col - m_new)
        ac = jnp.exp(b_end + m_prev - m_new)
        c_ref[h] = ac * c_aug + _dot_tn((kh * ws_col).astype(BF16), v_aug)
        m_ref[h] = jnp.broadcast_to(m_new, (1, LANES))
        sl = slice(h * ML_V_DIM, (h + 1) * ML_V_DIM)
        y = jax.nn.sigmoid(po_ref[:, sl]) * hh
        ymix_ref[:, GM_W + h * ML_V_DIM:GM_W + (h + 1) * ML_V_DIM] = _rms(y, mlo_ref[:, sl]).astype(BF16)

    o_ref[...] = x_ref[...] + _dot(ymix_ref[...], wo_ref[...])


def _mixer(x2d, proj, batch, gm_vnorm_g, w_spatial, bias_sp, conv_w, conv_b, gate_b, gm_out_g, ml_out_g, w_out):
    T, D = x2d.shape
    nc = T // batch // CHUNK
    tok = lambda w: pl.BlockSpec((CHUNK, w), lambda b, c: (b * nc + c, 0))
    pu, pv, pqk, pmv, po, pg = proj
    return pl.pallas_call(
        _mixer_kernel,
        grid=(batch, nc),
        in_specs=[tok(D)] + [tok(1024)] * 5 + [tok(LANES)] + [
            _resident((1, GM_W)), _resident(w_spatial.shape), _resident(bias_sp.shape),
            _resident(conv_w.shape), _resident((1, 2 * ML_QK_W)), _resident((1, LANES)),
            _resident((1, GM_W)), _resident((1, ML_W)), _resident(w_out.shape)],
        out_specs=tok(D),
        out_shape=jax.ShapeDtypeStruct((T, D), F32),
        scratch_shapes=[
            pltpu.VMEM((CHUNK, 2 * ML_QK_W), F32),
            pltpu.VMEM((ML_HEADS, ML_QK_DIM, ML_V_DIM + LANES), F32),
            pltpu.VMEM((ML_HEADS, 1, LANES), F32),
            pltpu.VMEM((CHUNK, D), BF16),
        ],
        compiler_params=pltpu.CompilerParams(
            dimension_semantics=("arbitrary", "arbitrary"), vmem_limit_bytes=VMEM_LIMIT),
        name="mixer",
    )(x2d, pu, pv, pqk, pmv, po, pg, gm_vnorm_g.reshape(1, GM_W), w_spatial, bias_sp, conv_w,
      conv_b.reshape(1, -1), gate_b, gm_out_g.reshape(1, GM_W), ml_out_g.reshape(1, ML_W), w_out)


def _top16(s, ids=None):
    n, tm = s.shape
    rid = lax.broadcasted_iota(jnp.int32, (n, tm), 0)
    slot = lax.broadcasted_iota(jnp.int32, (PEER_TOPK, tm), 0)
    vals = jnp.zeros((PEER_TOPK, tm), F32)
    picks = jnp.zeros((PEER_TOPK, tm), jnp.int32)
    for r in range(PEER_TOPK):
        m = jnp.max(s, axis=0, keepdims=True)
        pos = jnp.min(jnp.where(s == m, rid, n), axis=0, keepdims=True)
        hit = rid == pos
        pick = pos if ids is None else jnp.sum(jnp.where(hit, ids, 0), axis=0, keepdims=True)
        s = jnp.where(hit, NEG_INF, s)
        vals = jnp.where(slot == r, m, vals)
        picks = jnp.where(slot == r, pick, picks)
    return vals, picks


def _route_kernel(x_ref, g_ref, wq_ref, k1_ref, k2_ref, h2_ref, eidx_ref, gate_ref,
                  q_ref, eid_t_ref, gate_t_ref):
    h = pl.program_id(1)

    @pl.when(h == 0)
    def _():
        h2 = _rms(x_ref[...], g_ref[...]).astype(BF16)
        h2_ref[...] = h2
        q = _dot(h2, wq_ref[...])
        for hh in range(PEER_HEADS):
            q_ref[hh] = q[:, hh * PEER_QDIM:(hh + 1) * PEER_QDIM].astype(BF16)

    qh = q_ref[h]
    s1_all = _dot_nt(k1_ref[0], qh)
    s2_all = _dot_nt(k2_ref[0], qh)
    sub = lax.broadcasted_iota(jnp.int32, (SUBLANES, ROUTE_WIDTH), 0)
    slot2 = lax.broadcasted_iota(jnp.int32, (2 * PEER_TOPK, ROUTE_WIDTH), 0) >> 1
    for lt in range(qh.shape[0] // ROUTE_WIDTH):
        cols = slice(lt * ROUTE_WIDTH, (lt + 1) * ROUTE_WIDTH)
        v1, i1 = _top16(s1_all[:, cols])
        v2, i2 = _top16(s2_all[:, cols])
        blocks, id_blocks = [v1[0:1, :] + v2], [i1[0:1, :] * N_KEYS + i2]
        for i in range(1, SUBLANES):
            blocks.append(jnp.where(sub < PEER_TOPK // (i + 1), v1[i:i + 1, :] + v2[0:SUBLANES, :], NEG_INF))
            id_blocks.append(i1[i:i + 1, :] * N_KEYS + i2[0:SUBLANES, :])
        blocks.append(v1[SUBLANES:, :] + v2[0:1, :])
        id_blocks.append(i1[SUBLANES:, :] * N_KEYS + i2[0:1, :])
        sv, eid = _top16(jnp.concatenate(blocks, axis=0), jnp.concatenate(id_blocks, axis=0))
        e = jnp.exp(sv - jnp.max(sv, axis=0, keepdims=True))
        gate = e / jnp.sum(e, axis=0, keepdims=True)
        gate2 = jnp.zeros((2 * PEER_TOPK, ROUTE_WIDTH), F32)
        for r in range(PEER_TOPK):
            gate2 = jnp.where(slot2 == r, gate[r:r + 1, :], gate2)
        eid_t_ref[pl.ds(pl.multiple_of(h * PEER_TOPK, PEER_TOPK), PEER_TOPK), cols] = eid
        gate_t_ref[pl.ds(pl.multiple_of(h * 2 * PEER_TOPK, 2 * PEER_TOPK), 2 * PEER_TOPK), cols] = gate2

    @pl.when(h == PEER_HEADS - 1)
    def _():
        for lt in range(qh.shape[0] // LANES):
            cols = slice(lt * LANES, (lt + 1) * LANES)
            eidx_ref[cols, :] = eid_t_ref[:, cols].T
            for half in range(2):
                gate_ref[cols, half * LANES:(half + 1) * LANES] = (
                    gate_t_ref[half * LANES:(half + 1) * LANES, cols].T)


def _peer_route(x1, norm2_g, wq, k1p, k2p):
    T, D = x1.shape
    tm = ROUTE_TILE
    return pl.pallas_call(
        _route_kernel,
        grid=(T // tm, PEER_HEADS),
        in_specs=[pl.BlockSpec((tm, D), lambda i, h: (i, 0)), _resident((1, D)), _resident(wq.shape),
                  pl.BlockSpec((1, N_KEYS, PEER_QDIM), lambda i, h: (h, 0, 0)),
                  pl.BlockSpec((1, N_KEYS, PEER_QDIM), lambda i, h: (h, 0, 0))],
        out_specs=[pl.BlockSpec((tm, D), lambda i, h: (i, 0)),
                   pl.BlockSpec((tm, PEER_SLOTS), lambda i, h: (i, 0)),
                   pl.BlockSpec((tm, 2 * PEER_SLOTS), lambda i, h: (i, 0))],
        out_shape=[jax.ShapeDtypeStruct((T, D), BF16),
                   jax.ShapeDtypeStruct((T, PEER_SLOTS), jnp.int32),
                   jax.ShapeDtypeStruct((T, 2 * PEER_SLOTS), F32)],
        scratch_shapes=[pltpu.VMEM((PEER_HEADS, tm, PEER_QDIM), BF16),
                        pltpu.VMEM((PEER_SLOTS, tm), jnp.int32),
                        pltpu.VMEM((2 * PEER_SLOTS, tm), F32)],
        compiler_params=pltpu.CompilerParams(
            dimension_semantics=("parallel", "arbitrary"), vmem_limit_bytes=VMEM_LIMIT),
        name="peer_route",
    )(x1, norm2_g.reshape(1, D), wq, k1p, k2p)


def _experts_kernel(idx_ref, idx_next_ref, gate_ref, h2_ref, x1_ref, g_ref, tab_ref, tabw_ref, o_ref,
                    y_ref, *scratch):
    bufs, sem_ref = scratch[:GATHER_BUFS], scratch[GATHER_BUFS]
    half = h2_ref.shape[1] // 2
    i = pl.program_id(0)
    n = pl.num_programs(0)
    nu = tab_ref.shape[1] // 2

    def issue(ids_ref, tok):
        b = tok % GATHER_BUFS
        for r in range(PEER_SLOTS):
            pltpu.make_async_copy(
                tab_ref.at[ids_ref[0, 0, tok * PEER_SLOTS + r]],
                bufs[b].at[r // SUBLANES, :, pl.ds(r % SUBLANES, 1), :],
                sem_ref.at[b]).start(priority=r % 2)

    def wait(b):
        pltpu.make_async_copy(tabw_ref, bufs[b], sem_ref.at[b]).wait()

    lane = lax.broadcasted_iota(jnp.int32, (1, 2 * PEER_SLOTS), 1)
    even = (lane & 1) == 0

    def table_half(buf, s0):
        words = jnp.concatenate([buf[:, s0 + s].reshape(PEER_SLOTS, LANES) for s in range(nu)], axis=1)
        return pltpu.bitcast(words, BF16)

    def project(t):
        x = h2_ref[t:t + 1, :]
        x2 = jnp.concatenate([x[:, :half], x[:, half:]], axis=0)
        return _dot_nt(x2, table_half(bufs[t % GATHER_BUFS], 0))

    def activate(t, r):
        a = jnp.where(even, r[0:1, :], r[1:2, :])
        act = a + jnp.where(even, pltpu.roll(a, 2 * PEER_SLOTS - 1, 1), pltpu.roll(a, 1, 1))
        w = gate_ref[t:t + 1, :] * jax.nn.gelu(act)
        return jnp.concatenate([jnp.where(even, w, 0.0), jnp.where(even, 0.0, w)], axis=0).astype(BF16)

    def combine(t, lhs):
        y2 = _dot(lhs, table_half(bufs[t % GATHER_BUFS], nu))
        y_ref[t:t + 1, :half] = y2[0:1, :]
        y_ref[t:t + 1, half:] = y2[1:2, :]

    def issue_any(tok):
        if tok < GATHER_TILE:
            issue(idx_ref, tok)
        else:
            issue(idx_next_ref, tok - GATHER_TILE)

    g = GATHER_GROUP
    regions = GATHER_TILE // g
    group = lambda p: range(p * g, (p + 1) * g)

    @pl.when(i == 0)
    def _():
        for t in range(GATHER_AHEAD * g):
            issue(idx_ref, t)

    r, lhs = {}, {}
    for p in range(regions + 2):
        if p < regions:
            for t in group(p):
                wait(t % GATHER_BUFS)
            for t in group(p):
                r[t] = project(t)
        if 1 <= p <= regions:
            for t in group(p - 1):
                lhs[t] = activate(t, r.pop(t))
        if p >= 2:
            for t in group(p - 2):
                combine(t, lhs.pop(t))
        if p < regions:
            for t in group(p + GATHER_AHEAD):
                issue_any(t)

    o_ref[...] = _rms(x1_ref[...] + y_ref[...], g_ref[...])

    @pl.when(i == n - 1)
    def _():
        for t in range(GATHER_AHEAD * g):
            wait(t % GATHER_BUFS)


def _peer_experts(idx3, gate2, h2, x1, final_g, table):
    nt = idx3.shape[0] - 1
    T, D = x1.shape
    tt = GATHER_TILE
    srows = table.shape[1]
    rows = tt * PEER_SLOTS
    tok = pl.BlockSpec((tt, D), lambda i: (i, 0))
    buf = pltpu.VMEM((PEER_SLOTS // SUBLANES, srows, SUBLANES, LANES), jnp.uint32)
    return pl.pallas_call(
        _experts_kernel,
        grid=(nt,),
        in_specs=[pl.BlockSpec((1, 1, rows), lambda i: (i, 0, 0), memory_space=pltpu.SMEM),
                  pl.BlockSpec((1, 1, rows), lambda i: (i + 1, 0, 0), memory_space=pltpu.SMEM),
                  pl.BlockSpec((tt, 2 * PEER_SLOTS), lambda i: (i, 0)),
                  tok, tok, _resident((1, D)),
                  pl.BlockSpec(memory_space=pl.ANY),
                  pl.BlockSpec(memory_space=pl.ANY)],
        out_specs=tok,
        out_shape=jax.ShapeDtypeStruct((T, D), F32),
        scratch_shapes=[pltpu.VMEM((tt, D), F32)] + [buf] * GATHER_BUFS
                       + [pltpu.SemaphoreType.DMA((GATHER_BUFS,))],
        compiler_params=pltpu.CompilerParams(
            dimension_semantics=("arbitrary",), vmem_limit_bytes=VMEM_LIMIT),
        name="peer_experts",
    )(idx3, idx3, gate2, h2, x1, final_g.reshape(1, D), table, jnp.zeros(buf.shape, jnp.uint32))


def _pack_kernel(u_ref, v_ref, o_ref):
    rows, d = u_ref.shape
    half = d // 2
    nu = half // LANES
    for src, off in ((u_ref, 0), (v_ref, nu)):
        w = src[...].astype(BF16).astype(F32)
        bits = lax.bitcast_convert_type(w, jnp.uint32)
        words = (bits[:, :half] >> 16) | bits[:, half:]
        for s in range(nu):
            o_ref[pl.ds(off + s, rows, stride=2 * nu), :] = words[:, s * LANES:(s + 1) * LANES]


def _pack_table(peer_u, peer_v):
    n, d = peer_u.shape
    srows = d // LANES
    rt = PACK_TILE
    out = pl.pallas_call(
        _pack_kernel,
        grid=(n // rt,),
        in_specs=[pl.BlockSpec((rt, d), lambda i: (i, 0))] * 2,
        out_specs=pl.BlockSpec((rt * srows, LANES), lambda i: (i, 0)),
        out_shape=jax.ShapeDtypeStruct((n * srows, LANES), jnp.uint32),
        compiler_params=pltpu.CompilerParams(
            dimension_semantics=("parallel",), vmem_limit_bytes=VMEM_LIMIT),
        name="pack_table",
    )(peer_u, peer_v)
    return out.reshape(n, srows, 1, LANES)


def kernel(x, norm1_g, w_in, gm_vnorm_g, w_spatial, b_spatial, ml_conv_w, ml_conv_b, ml_b_i, ml_b_f,
           gm_out_g, ml_out_g, w_out, norm2_g, peer_wq, peer_k1, peer_k2, peer_u, peer_v, final_g):
    B, S, D = x.shape
    T = B * S
    assert S % CHUNK == 0 and T % max(PROJ_TILE, ROUTE_TILE, GATHER_TILE) == 0 and D % (2 * LANES) == 0
    x2d = x.reshape(T, D)

    n_main = 2 * GM_W + 2 * ML_QK_W + 2 * ML_W
    w_main = w_in[:, :n_main].astype(BF16)
    w_gate = jnp.pad(w_in[:, n_main:], ((0, 0), (0, LANES - 2 * ML_HEADS))).astype(BF16)
    gate_b = jnp.pad(jnp.concatenate([ml_b_i, ml_b_f]), (0, LANES - 2 * ML_HEADS)).reshape(1, LANES)
    bias_sp = jnp.broadcast_to(b_spatial[:, :, None], (GM_HEADS, CHUNK, GM_HEAD_DIM))
    k1p = jnp.pad(peer_k1, ((0, 0), (0, 0), (0, PEER_HALF))).astype(BF16)
    k2p = jnp.pad(peer_k2, ((0, 0), (0, 0), (PEER_HALF, 0))).astype(BF16)
    table = _pack_table(peer_u, peer_v)

    proj = _in_proj(x2d, norm1_g, w_main, w_gate)
    x1 = _mixer(x2d, proj, B, gm_vnorm_g, w_spatial, bias_sp, ml_conv_w, ml_conv_b, gate_b,
                gm_out_g, ml_out_g, w_out.astype(BF16))
    h2, eidx, gate2 = _peer_route(x1, norm2_g, peer_wq.astype(BF16), k1p, k2p)
    idx3 = eidx.reshape(T // GATHER_TILE, 1, GATHER_TILE * PEER_SLOTS)
    idx3 = jnp.pad(idx3, ((0, 1), (0, 0), (0, 0)))
    out = _peer_experts(idx3, gate2, h2, x1, final_g, table)
    return out.reshape(B, S, D)
```

```python
import functools

import jax
import jax.numpy as jnp
from jax import lax
from jax.experimental import pallas as pl
from jax.experimental.pallas import tpu as pltpu

EPS = 1e-6
GM_HEADS = 8
GM_HEAD_DIM = 128
GM_W = GM_HEADS * GM_HEAD_DIM
CHUNK = 128
ML_HEADS = 4
ML_QK_DIM = 128
ML_V_DIM = 256
ML_QK_W = ML_HEADS * ML_QK_DIM
ML_W = ML_HEADS * ML_V_DIM
ML_CONV = 4
PEER_HEADS = 8
PEER_QDIM = 128
PEER_HALF = PEER_QDIM // 2
N_KEYS = 128
PEER_TOPK = 16
PEER_SLOTS = PEER_HEADS * PEER_TOPK

LANES = 128
SUBLANES = 8
VMEM_LIMIT = 56 * 1024 * 1024

PACK_TILE = 256
PROJ_TILE = 256
MIX_CHUNKS = 2
ROUTE_TILE = 256
ROUTE_WIDTH = 256
ROUTE_HEADS = 8
GATHER_TILE = 16
GATHER_BUFS = 16
GATHER_GROUP = 2
GATHER_AHEAD = 4
assert GATHER_TILE % GATHER_BUFS == 0 and GATHER_BUFS >= (GATHER_AHEAD + 3) * GATHER_GROUP

F32 = jnp.float32
BF16 = jnp.bfloat16
NEG_INF = float("-inf")


def _rms(x, g):
    return x * lax.rsqrt(jnp.mean(x * x, axis=-1, keepdims=True) + EPS) * g


def _dot(a, b):
    return jnp.dot(a, b, preferred_element_type=F32)


def _dot_nt(a, b):
    return lax.dot_general(a, b, (((1,), (1,)), ((), ())), preferred_element_type=F32)


def _dot_tn(a, b):
    return lax.dot_general(a, b, (((0,), (0,)), ((), ())), preferred_element_type=F32)


def _resident(shape):
    nd = len(shape)
    return pl.BlockSpec(shape, lambda *_: (0,) * nd, pipeline_mode=pl.Buffered(1))


def _in_proj_kernel(x_ref, g_ref, wm_ref, wg_ref, pu_ref, pv_ref, pqk_ref, pmv_ref, po_ref, pg_ref):
    h = _rms(x_ref[...], g_ref[...]).astype(BF16)
    for j, o_ref in enumerate((pu_ref, pv_ref, pqk_ref, pmv_ref, po_ref)):
        o_ref[...] = _dot(h, wm_ref[:, j * 1024:(j + 1) * 1024])
    pg_ref[...] = _dot(h, wg_ref[...])


def _in_proj(x2d, norm1_g, w_main, w_gate):
    T, D = x2d.shape
    tm = PROJ_TILE
    tok = lambda w: pl.BlockSpec((tm, w), lambda i: (i, 0))
    outs = [jax.ShapeDtypeStruct((T, 1024), F32)] * 5 + [jax.ShapeDtypeStruct((T, LANES), F32)]
    return pl.pallas_call(
        _in_proj_kernel,
        grid=(T // tm,),
        in_specs=[tok(D), _resident((1, D)), _resident(w_main.shape), _resident(w_gate.shape)],
        out_specs=[tok(1024)] * 5 + [tok(LANES)],
        out_shape=outs,
        compiler_params=pltpu.CompilerParams(
            dimension_semantics=("parallel",), vmem_limit_bytes=VMEM_LIMIT),
        name="in_proj",
    )(x2d, norm1_g.reshape(1, D), w_main, w_gate)


def _mixer_kernel(x_ref, pu_ref, pv_ref, pqk_ref, pmv_ref, po_ref, pg_ref,
                  gvn_ref, ws_ref, bsp_ref, cw_ref, cb_ref, bg_ref, gmo_ref, mlo_ref, wo_ref,
                  o_ref, prev_ref, c_ref, m_ref, ymix_ref):
    L = CHUNK

    @pl.when(pl.program_id(1) == 0)
    def _():
        prev_ref[...] = jnp.zeros_like(prev_ref)
        c_ref[...] = jnp.zeros_like(c_ref)
        m_ref[...] = jnp.zeros_like(m_ref)

    row = lax.broadcasted_iota(jnp.int32, (L, L), 0)
    col = lax.broadcasted_iota(jnp.int32, (L, L), 1)
    causal = col <= row
    rows_w = lax.broadcasted_iota(jnp.int32, (L, 2 * ML_QK_W), 0)
    ones_col = jnp.where(lax.broadcasted_iota(jnp.int32, (L, LANES), 1) == 0, 1.0, 0.0).astype(BF16)
    w_sp = [jnp.where(causal, ws_ref[h], 0.0).astype(BF16) for h in range(GM_HEADS)]

    prev = prev_ref[...]
    for c in range(MIX_CHUNKS):
        rs = slice(c * L, (c + 1) * L)

        u = jax.nn.gelu(pu_ref[rs, :])
        vg = _rms(jax.nn.gelu(pv_ref[rs, :]), gvn_ref[...]).astype(BF16)
        for h in range(GM_HEADS):
            sl = slice(h * GM_HEAD_DIM, (h + 1) * GM_HEAD_DIM)
            mixed = _dot(w_sp[h], vg[:, sl]) + bsp_ref[h]
            y = u[:, sl] * mixed
            ymix_ref[rs, sl] = _rms(y, gmo_ref[:, sl]).astype(BF16)

        xqk = pqk_ref[rs, :]
        acc = cb_ref[...] + cw_ref[ML_CONV - 1:ML_CONV, :] * xqk
        for j in range(1, ML_CONV):
            shifted = jnp.where(rows_w >= j, pltpu.roll(xqk, j, 0), pltpu.roll(prev, j, 0))
            acc = acc + cw_ref[ML_CONV - 1 - j:ML_CONV - j, :] * shifted
        prev = xqk
        qk = acc * jax.nn.sigmoid(acc)
        q_all = qk[:, :ML_QK_W].astype(BF16)
        k_all = qk[:, ML_QK_W:] * (ML_QK_DIM ** -0.5)

        gt = pg_ref[rs, :] + bg_ref[...]
        lf = jnp.minimum(gt, 0.0) - jnp.log1p(jnp.exp(-jnp.abs(gt)))
        bcum = lf
        sh = 1
        while sh < L:
            bcum = bcum + jnp.where(row >= sh, pltpu.roll(bcum, sh, 0), 0.0)
            sh *= 2
        gt_t = gt.T
        bcum_t = bcum.T

        for h in range(ML_HEADS):
            ig_col = gt[:, h:h + 1]
            b_col = bcum[:, ML_HEADS + h:ML_HEADS + h + 1]
            ig_row = gt_t[h:h + 1, :]
            b_row = bcum_t[ML_HEADS + h:ML_HEADS + h + 1, :]
            m_prev = m_ref[h][:, 0:1]
            dmat = jnp.where(causal, b_col - b_row + ig_row, NEG_INF)
            inter = b_col + m_prev
            m_t = jnp.maximum(inter, jnp.max(dmat, axis=1, keepdims=True))
            w_intra = jnp.exp(dmat - m_t)
            a_inter = jnp.exp(inter - m_t)
            qh = q_all[:, h * ML_QK_DIM:(h + 1) * ML_QK_DIM]
            kh = k_all[:, h * ML_QK_DIM:(h + 1) * ML_QK_DIM]
            vh = pmv_ref[rs, h * ML_V_DIM:(h + 1) * ML_V_DIM].astype(BF16)
            v_aug = jnp.concatenate([vh, ones_col], axis=1)
            c_aug = c_ref[h]
            s = _dot_nt(qh, kh.astype(BF16)) * w_intra
            num_aug = _dot(s.astype(BF16), v_aug) + a_inter * _dot(qh, c_aug.astype(BF16))
            num = num_aug[:, :ML_V_DIM]
            den = num_aug[:, ML_V_DIM:ML_V_DIM + 1]
            hh = num / jnp.maximum(jnp.abs(den), jnp.exp(-m_t))
            b_end = b_col[L - 1:L, :]
            g_col = b_end - b_col + ig_col
            m_new = jnp.maximum(b_end + m_prev, jnp.max(g_col, axis=0, keepdims=True))
            ws_col = jnp.exp(g_col - m_new)
            ac = jnp.exp(b_end + m_prev - m_new)
            c_ref[h] = ac * c_aug + _dot_tn((kh * ws_col).astype(BF16), v_aug)
            m_ref[h] = jnp.broadcast_to(m_new, (1, LANES))
            sl = slice(h * ML_V_DIM, (h + 1) * ML_V_DIM)
            y = jax.nn.sigmoid(po_ref[rs, sl]) * hh
            ymix_ref[rs, GM_W + h * ML_V_DIM:GM_W + (h + 1) * ML_V_DIM] = (
                _rms(y, mlo_ref[:, sl]).astype(BF16))

    prev_ref[...] = prev
    o_ref[...] = x_ref[...] + _dot(ymix_ref[...], wo_ref[...])


def _mixer(x2d, proj, batch, gm_vnorm_g, w_spatial, bias_sp, conv_w, conv_b, gate_b, gm_out_g, ml_out_g, w_out):
    T, D = x2d.shape
    tm = MIX_CHUNKS * CHUNK
    nc = T // batch // tm
    tok = lambda w: pl.BlockSpec((tm, w), lambda b, c: (b * nc + c, 0))
    pu, pv, pqk, pmv, po, pg = proj
    return pl.pallas_call(
        _mixer_kernel,
        grid=(batch, nc),
        in_specs=[tok(D)] + [tok(1024)] * 5 + [tok(LANES)] + [
            _resident((1, GM_W)), _resident(w_spatial.shape), _resident(bias_sp.shape),
            _resident(conv_w.shape), _resident((1, 2 * ML_QK_W)), _resident((1, LANES)),
            _resident((1, GM_W)), _resident((1, ML_W)), _resident(w_out.shape)],
        out_specs=tok(D),
        out_shape=jax.ShapeDtypeStruct((T, D), F32),
        scratch_shapes=[
            pltpu.VMEM((CHUNK, 2 * ML_QK_W), F32),
            pltpu.VMEM((ML_HEADS, ML_QK_DIM, ML_V_DIM + LANES), F32),
            pltpu.VMEM((ML_HEADS, 1, LANES), F32),
            pltpu.VMEM((tm, D), BF16),
        ],
        compiler_params=pltpu.CompilerParams(
            dimension_semantics=("arbitrary", "arbitrary"), vmem_limit_bytes=VMEM_LIMIT),
        name="mixer",
    )(x2d, pu, pv, pqk, pmv, po, pg, gm_vnorm_g.reshape(1, GM_W), w_spatial, bias_sp, conv_w,
      conv_b.reshape(1, -1), gate_b, gm_out_g.reshape(1, GM_W), ml_out_g.reshape(1, ML_W), w_out)


def _top16(s, ids=None):
    n, tm = s.shape
    rid = lax.broadcasted_iota(jnp.int32, (n, tm), 0)
    slot = lax.broadcasted_iota(jnp.int32, (PEER_TOPK, tm), 0)
    vals = jnp.zeros((PEER_TOPK, tm), F32)
    picks = jnp.zeros((PEER_TOPK, tm), jnp.int32)
    for r in range(PEER_TOPK):
        m = jnp.max(s, axis=0, keepdims=True)
        pos = jnp.min(jnp.where(s == m, rid, n), axis=0, keepdims=True)
        hit = rid == pos
        pick = pos if ids is None else jnp.sum(jnp.where(hit, ids, 0), axis=0, keepdims=True)
        s = jnp.where(hit, NEG_INF, s)
        vals = jnp.where(slot == r, m, vals)
        picks = jnp.where(slot == r, pick, picks)
    return vals, picks


def _route_kernel(x_ref, g_ref, wq_ref, k1_ref, k2_ref, h2_ref, eidx_ref, gate_ref,
                  q_ref, eid_t_ref, gate_t_ref):
    h = pl.program_id(1)

    @pl.when(h == 0)
    def _():
        h2 = _rms(x_ref[...], g_ref[...]).astype(BF16)
        h2_ref[...] = h2
        q = _dot(h2, wq_ref[...])
        for hh in range(PEER_HEADS):
            q_ref[hh] = q[:, hh * PEER_QDIM:(hh + 1) * PEER_QDIM].astype(BF16)

    sub = lax.broadcasted_iota(jnp.int32, (SUBLANES, ROUTE_WIDTH), 0)
    slot2 = lax.broadcasted_iota(jnp.int32, (2 * PEER_TOPK, ROUTE_WIDTH), 0) >> 1
    for hh in range(ROUTE_HEADS):
        head = h * ROUTE_HEADS + hh
        qh = q_ref[head]
        s1_all = _dot_nt(k1_ref[hh], qh)
        s2_all = _dot_nt(k2_ref[hh], qh)
        for lt in range(qh.shape[0] // ROUTE_WIDTH):
            cols = slice(lt * ROUTE_WIDTH, (lt + 1) * ROUTE_WIDTH)
            v1, i1 = _top16(s1_all[:, cols])
            v2, i2 = _top16(s2_all[:, cols])
            blocks, id_blocks = [v1[0:1, :] + v2], [i1[0:1, :] * N_KEYS + i2]
            for i in range(1, SUBLANES):
                blocks.append(jnp.where(sub < PEER_TOPK // (i + 1),
                                        v1[i:i + 1, :] + v2[0:SUBLANES, :], NEG_INF))
                id_blocks.append(i1[i:i + 1, :] * N_KEYS + i2[0:SUBLANES, :])
            blocks.append(v1[SUBLANES:, :] + v2[0:1, :])
            id_blocks.append(i1[SUBLANES:, :] * N_KEYS + i2[0:1, :])
            sv, eid = _top16(jnp.concatenate(blocks, axis=0), jnp.concatenate(id_blocks, axis=0))
            e = jnp.exp(sv - jnp.max(sv, axis=0, keepdims=True))
            gate = e / jnp.sum(e, axis=0, keepdims=True)
            gate2 = jnp.zeros((2 * PEER_TOPK, ROUTE_WIDTH), F32)
            for r in range(PEER_TOPK):
                gate2 = jnp.where(slot2 == r, gate[r:r + 1, :], gate2)
            eid_t_ref[pl.ds(pl.multiple_of(head * PEER_TOPK, PEER_TOPK), PEER_TOPK), cols] = eid
            gate_t_ref[pl.ds(pl.multiple_of(head * 2 * PEER_TOPK, 2 * PEER_TOPK), 2 * PEER_TOPK),
                       cols] = gate2

    @pl.when(h == PEER_HEADS // ROUTE_HEADS - 1)
    def _():
        for lt in range(qh.shape[0] // LANES):
            cols = slice(lt * LANES, (lt + 1) * LANES)
            eidx_ref[cols, :] = eid_t_ref[:, cols].T
            for half in range(2):
                gate_ref[cols, half * LANES:(half + 1) * LANES] = (
                    gate_t_ref[half * LANES:(half + 1) * LANES, cols].T)


def _peer_route(x1, norm2_g, wq, k1p, k2p):
    T, D = x1.shape
    tm = ROUTE_TILE
    return pl.pallas_call(
        _route_kernel,
        grid=(T // tm, PEER_HEADS // ROUTE_HEADS),
        in_specs=[pl.BlockSpec((tm, D), lambda i, h: (i, 0)), _resident((1, D)), _resident(wq.shape),
                  pl.BlockSpec((ROUTE_HEADS, N_KEYS, PEER_QDIM), lambda i, h: (h, 0, 0)),
                  pl.BlockSpec((ROUTE_HEADS, N_KEYS, PEER_QDIM), lambda i, h: (h, 0, 0))],
        out_specs=[pl.BlockSpec((tm, D), lambda i, h: (i, 0)),
                   pl.BlockSpec((tm, PEER_SLOTS), lambda i, h: (i, 0)),
                   pl.BlockSpec((tm, 2 * PEER_SLOTS), lambda i, h: (i, 0))],
        out_shape=[jax.ShapeDtypeStruct((T, D), BF16),
                   jax.ShapeDtypeStruct((T, PEER_SLOTS), jnp.int32),
                   jax.ShapeDtypeStruct((T, 2 * PEER_SLOTS), F32)],
        scratch_shapes=[pltpu.VMEM((PEER_HEADS, tm, PEER_QDIM), BF16),
                        pltpu.VMEM((PEER_SLOTS, tm), jnp.int32),
                        pltpu.VMEM((2 * PEER_SLOTS, tm), F32)],
        compiler_params=pltpu.CompilerParams(
            dimension_semantics=("parallel", "arbitrary"), vmem_limit_bytes=VMEM_LIMIT),
        name="peer_route",
    )(x1, norm2_g.reshape(1, D), wq, k1p, k2p)


def _experts_kernel(idx_ref, idx_next_ref, gate_ref, h2_ref, x1_ref, g_ref, tab_ref, tabw_ref, o_ref,
                    y_ref, *scratch):
    bufs, sem_ref = scratch[:GATHER_BUFS], scratch[GATHER_BUFS]
    half = h2_ref.shape[1] // 2
    i = pl.program_id(0)
    n = pl.num_programs(0)
    nu = tab_ref.shape[1] // 2

    def issue(ids_ref, tok):
        b = tok % GATHER_BUFS
        for r in range(PEER_SLOTS):
            pltpu.make_async_copy(
                tab_ref.at[ids_ref[0, 0, tok * PEER_SLOTS + r]],
                bufs[b].at[r // SUBLANES, :, pl.ds(r % SUBLANES, 1), :],
                sem_ref.at[b]).start(priority=r % 2)

    def wait(b):
        pltpu.make_async_copy(tabw_ref, bufs[b], sem_ref.at[b]).wait()

    lane = lax.broadcasted_iota(jnp.int32, (1, 2 * PEER_SLOTS), 1)
    even = (lane & 1) == 0

    def table_half(buf, s0):
        words = jnp.concatenate([buf[:, s0 + s].reshape(PEER_SLOTS, LANES) for s in range(nu)], axis=1)
        return pltpu.bitcast(words, BF16)

    def project(t):
        x = h2_ref[t:t + 1, :]
        x2 = jnp.concatenate([x[:, :half], x[:, half:]], axis=0)
        return _dot_nt(x2, table_half(bufs[t % GATHER_BUFS], 0))

    def activate(t, r):
        a = jnp.where(even, r[0:1, :], r[1:2, :])
        act = a + jnp.where(even, pltpu.roll(a, 2 * PEER_SLOTS - 1, 1), pltpu.roll(a, 1, 1))
        w = gate_ref[t:t + 1, :] * jax.nn.gelu(act)
        return jnp.concatenate([jnp.where(even, w, 0.0), jnp.where(even, 0.0, w)], axis=0).astype(BF16)

    def combine(t, lhs):
        y2 = _dot(lhs, table_half(bufs[t % GATHER_BUFS], nu))
        y_ref[t:t + 1, :half] = y2[0:1, :]
        y_ref[t:t + 1, half:] = y2[1:2, :]

    def issue_any(tok):
        if tok < GATHER_TILE:
            issue(idx_ref, tok)
        else:
            issue(idx_next_ref, tok - GATHER_TILE)

    g = GATHER_GROUP
    regions = GATHER_TILE // g
    group = lambda p: range(p * g, (p + 1) * g)

    @pl.when(i == 0)
    def _():
        for t in range(GATHER_AHEAD * g):
            issue(idx_ref, t)

    r, lhs = {}, {}
    for p in range(regions + 2):
        if p < regions:
            for t in group(p):
                wait(t % GATHER_BUFS)
            for t in group(p):
                r[t] = project(t)
        if 1 <= p <= regions:
            for t in group(p - 1):
                lhs[t] = activate(t, r.pop(t))
        if p >= 2:
            for t in group(p - 2):
                combine(t, lhs.pop(t))
        if p < regions:
            for t in group(p + GATHER_AHEAD):
                issue_any(t)

    o_ref[...] = _rms(x1_ref[...] + y_ref[...], g_ref[...])

    @pl.when(i == n - 1)
    def _():
        for t in range(GATHER_AHEAD * g):
            wait(t % GATHER_BUFS)


def _peer_experts(idx3, gate2, h2, x1, final_g, table):
    nt = idx3.shape[0] - 1
    T, D = x1.shape
    tt = GATHER_TILE
    srows = table.shape[1]
    rows = tt * PEER_SLOTS
    tok = pl.BlockSpec((tt, D), lambda i: (i, 0))
    buf = pltpu.VMEM((PEER_SLOTS // SUBLANES, srows, SUBLANES, LANES), jnp.uint32)
    return pl.pallas_call(
        _experts_kernel,
        grid=(nt,),
        in_specs=[pl.BlockSpec((1, 1, rows), lambda i: (i, 0, 0), memory_space=pltpu.SMEM),
                  pl.BlockSpec((1, 1, rows), lambda i: (i + 1, 0, 0), memory_space=pltpu.SMEM),
                  pl.BlockSpec((tt, 2 * PEER_SLOTS), lambda i: (i, 0)),
                  tok, tok, _resident((1, D)),
                  pl.BlockSpec(memory_space=pl.ANY),
                  pl.BlockSpec(memory_space=pl.ANY)],
        out_specs=tok,
        out_shape=jax.ShapeDtypeStruct((T, D), F32),
        scratch_shapes=[pltpu.VMEM((tt, D), F32)] + [buf] * GATHER_BUFS
                       + [pltpu.SemaphoreType.DMA((GATHER_BUFS,))],
        compiler_params=pltpu.CompilerParams(
            dimension_semantics=("arbitrary",), vmem_limit_bytes=VMEM_LIMIT),
        name="peer_experts",
    )(idx3, idx3, gate2, h2, x1, final_g.reshape(1, D), table, jnp.zeros(buf.shape, jnp.uint32))


def _pack_kernel(u_ref, v_ref, o_ref):
    rows, d = u_ref.shape
    half = d // 2
    nu = half // LANES
    for src, off in ((u_ref, 0), (v_ref, nu)):
        w = src[...].astype(BF16).astype(F32)
        bits = lax.bitcast_convert_type(w, jnp.uint32)
        words = (bits[:, :half] >> 16) | bits[:, half:]
        for s in range(nu):
            o_ref[pl.ds(off + s, rows, stride=2 * nu), :] = words[:, s * LANES:(s + 1) * LANES]


def _pack_table(peer_u, peer_v):
    n, d = peer_u.shape
    srows = d // LANES
    rt = PACK_TILE
    out = pl.pallas_call(
        _pack_kernel,
        grid=(n // rt,),
        in_specs=[pl.BlockSpec((rt, d), lambda i: (i, 0))] * 2,
        out_specs=pl.BlockSpec((rt * srows, LANES), lambda i: (i, 0)),
        out_shape=jax.ShapeDtypeStruct((n * srows, LANES), jnp.uint32),
        compiler_params=pltpu.CompilerParams(
            dimension_semantics=("parallel",), vmem_limit_bytes=VMEM_LIMIT),
        name="pack_table",
    )(peer_u, peer_v)
    return out.reshape(n, srows, 1, LANES)


def kernel(x, norm1_g, w_in, gm_vnorm_g, w_spatial, b_spatial, ml_conv_w, ml_conv_b, ml_b_i, ml_b_f,
           gm_out_g, ml_out_g, w_out, norm2_g, peer_wq, peer_k1, peer_k2, peer_u, peer_v, final_g):
    B, S, D = x.shape
    T = B * S
    assert S % (MIX_CHUNKS * CHUNK) == 0 and D % (2 * LANES) == 0
    assert T % max(PROJ_TILE, ROUTE_TILE, GATHER_TILE) == 0
    x2d = x.reshape(T, D)

    n_main = 2 * GM_W + 2 * ML_QK_W + 2 * ML_W
    w_main = w_in[:, :n_main].astype(BF16)
    w_gate = jnp.pad(w_in[:, n_main:], ((0, 0), (0, LANES - 2 * ML_HEADS))).astype(BF16)
    gate_b = jnp.pad(jnp.concatenate([ml_b_i, ml_b_f]), (0, LANES - 2 * ML_HEADS)).reshape(1, LANES)
    bias_sp = jnp.broadcast_to(b_spatial[:, :, None], (GM_HEADS, CHUNK, GM_HEAD_DIM))
    k1p = jnp.pad(peer_k1, ((0, 0), (0, 0), (0, PEER_HALF))).astype(BF16)
    k2p = jnp.pad(peer_k2, ((0, 0), (0, 0), (PEER_HALF, 0))).astype(BF16)
    table = _pack_table(peer_u, peer_v)

    proj = _in_proj(x2d, norm1_g, w_main, w_gate)
    x1 = _mixer(x2d, proj, B, gm_vnorm_g, w_spatial, bias_sp, ml_conv_w, ml_conv_b, gate_b,
                gm_out_g, ml_out_g, w_out.astype(BF16))
    h2, eidx, gate2 = _peer_route(x1, norm2_g, peer_wq.astype(BF16), k1p, k2p)
    idx3 = eidx.reshape(T // GATHER_TILE, 1, GATHER_TILE * PEER_SLOTS)
    idx3 = jnp.pad(idx3, ((0, 1), (0, 0), (0, 0)))
    out = _peer_experts(idx3, gate2, h2, x1, final_g, table)
    return out.reshape(B, S, D)
```
